```python
import math, functools
import jax, jax.numpy as jnp
from jax import lax
import numpy as np

D_MODEL = 2048
BATCH = 1
SEQ = 8192
DEPTH = 1
DEC_BATCH = 32
DEC_SEQ = 1
PAST_LEN = 16384
PAGE_SIZE = 128

HEAD_DIM = 128
FOX_HEADS = D_MODEL // 256
DIFF_HEADS = D_MODEL // 512
FOX_W = FOX_HEADS * HEAD_DIM
DIFF_QK_W = DIFF_HEADS * 2 * HEAD_DIM
DIFF_V_DIM = 2 * HEAD_DIM
DIFF_V_W = DIFF_HEADS * DIFF_V_DIM
IN_WIDTHS = (FOX_W, FOX_W, FOX_W, FOX_HEADS, DIFF_QK_W, DIFF_QK_W, DIFF_V_W, D_MODEL, D_MODEL)
IN_W = sum(IN_WIDTHS)
ROPE_THETA = 10000.0
Q_BLOCK = 128
ATTN_SCALE = HEAD_DIM ** -0.5
PEER_HEADS = 8
PEER_KEY_DIM = 256
PEER_HALF = PEER_KEY_DIM // 2
N_KEYS = 128
N_EXPERTS = N_KEYS * N_KEYS
PEER_TOPK = 16
PEER_BLOCK = 128
LN_EPS = 1e-5
DEEPNORM_ALPHA = (2 * DEPTH) ** 0.25
DEEPNORM_BETA = (8 * DEPTH) ** -0.25

kernel_name = 'hybrid_fox_diff_peer_step'


def split_points():
    pts, acc = [], 0
    for w in IN_WIDTHS[:-1]:
        acc += w
        pts.append(acc)
    return pts


def layer_norm(x, g=None, b=None):
    xf = x.astype(jnp.float32)
    mu = jnp.mean(xf, axis=-1, keepdims=True)
    var = jnp.mean(jnp.square(xf - mu), axis=-1, keepdims=True)
    y = (xf - mu) * lax.rsqrt(var + LN_EPS)
    if g is not None:
        y = y * g.astype(jnp.float32) + b.astype(jnp.float32)
    return y.astype(x.dtype)


def rms_norm(x, w):
    xf = x.astype(jnp.float32)
    y = xf * lax.rsqrt(jnp.mean(jnp.square(xf), axis=-1, keepdims=True) + LN_EPS)
    return (y * w.astype(jnp.float32)).astype(x.dtype)


def rope(x, pos):
    half = HEAD_DIM // 2
    inv = ROPE_THETA ** (-jnp.arange(half, dtype=jnp.float32) / half)
    ang = pos.astype(jnp.float32)[:, None] * inv[None, :]
    cos = jnp.cos(ang)[:, None, :]
    sin = jnp.sin(ang)[:, None, :]
    x1 = x[..., :half].astype(jnp.float32)
    x2 = x[..., half:].astype(jnp.float32)
    return jnp.concatenate([x1 * cos - x2 * sin, x2 * cos + x1 * sin], axis=-1).astype(x.dtype)


def adaln(c, w_ada, b_ada):
    m = jax.nn.silu(c) @ w_ada + b_ada
    return jnp.split(m[:, None, :], 6, axis=-1)


def fox_block(q, k, v, cq, ck, mask):
    s = jnp.einsum('bqhd,bkhd->bhqk', q, k).astype(jnp.float32) * ATTN_SCALE
    s = s + (jnp.swapaxes(cq, 1, 2)[..., :, None] - jnp.swapaxes(ck, 1, 2)[..., None, :])
    s = jnp.where(mask, s, -jnp.inf)
    p = jax.nn.softmax(s, axis=-1)
    return jnp.einsum('bhqk,bkhd->bqhd', p.astype(v.dtype), v)


def diff_block(q, k, v, lam, mask):
    s = jnp.einsum('bqhnd,bkhnd->bhnqk', q, k).astype(jnp.float32) * ATTN_SCALE
    s = jnp.where(mask, s, -jnp.inf)
    p = jax.nn.softmax(s, axis=-1)
    a = p[:, :, 0] - lam * p[:, :, 1]
    return jnp.einsum('bhqk,bkhe->bqhe', a.astype(v.dtype), v)


def prompt_core(fq, fk, fv, logf, dq, dk, dv, lam):
    B, S = fq.shape[0], fq.shape[1]
    cum = jnp.cumsum(logf, axis=1)
    n_blocks = S // Q_BLOCK
    kpos = jnp.arange(S)

    def one_block(i):
        start = i * Q_BLOCK
        qpos = start + jnp.arange(Q_BLOCK)
        mask = kpos[None, :] <= qpos[:, None]
        fq_i = lax.dynamic_slice_in_dim(fq, start, Q_BLOCK, axis=1)
        cq_i = lax.dynamic_slice_in_dim(cum, start, Q_BLOCK, axis=1)
        dq_i = lax.dynamic_slice_in_dim(dq, start, Q_BLOCK, axis=1)
        return (fox_block(fq_i, fk, fv, cq_i, cum, mask),
                diff_block(dq_i, dk, dv, lam, mask))

    fo, do = lax.map(one_block, jnp.arange(n_blocks))
    fo = jnp.moveaxis(fo, 0, 1).reshape(B, S, FOX_HEADS, HEAD_DIM)
    do = jnp.moveaxis(do, 0, 1).reshape(B, S, DIFF_HEADS, DIFF_V_DIM)
    return fo, do


def sample_core(l, cache_fox_k, cache_fox_v, cache_fox_logf, cache_diff_k, cache_diff_v,
                page_table, fq, fk, fv, logf, dq, dk, dv, lam):
    s_new = fq.shape[1]
    n_past = page_table.shape[1] * PAGE_SIZE
    kpos = jnp.arange(n_past + s_new)
    qpos = n_past + jnp.arange(s_new)
    mask = kpos[None, :] <= qpos[:, None]

    def gather_past(cache, pages):
        g = cache[l, pages]
        return g.reshape((n_past,) + g.shape[2:])

    def one_seq(args):
        pages, fq_b, fk_b, fv_b, lf_b, dq_b, dk_b, dv_b = args
        k_all = jnp.concatenate([gather_past(cache_fox_k, pages).astype(fk_b.dtype), fk_b], axis=0)
        v_all = jnp.concatenate([gather_past(cache_fox_v, pages).astype(fv_b.dtype), fv_b], axis=0)
        lf_all = jnp.concatenate([gather_past(cache_fox_logf, pages).astype(jnp.float32), lf_b], axis=0)
        cum = jnp.cumsum(lf_all, axis=0)
        fo = fox_block(fq_b[None], k_all[None], v_all[None], cum[n_past:][None], cum[None], mask)[0]
        dk_past = gather_past(cache_diff_k, pages).reshape(n_past, DIFF_HEADS, 2, HEAD_DIM)
        dk_all = jnp.concatenate([dk_past.astype(dk_b.dtype), dk_b], axis=0)
        dv_all = jnp.concatenate([gather_past(cache_diff_v, pages).astype(dv_b.dtype), dv_b], axis=0)
        do = diff_block(dq_b[None], dk_all[None], dv_all[None], lam, mask)[0]
        return fo, do

    return lax.map(one_seq, (page_table, fq, fk, fv, logf, dq, dk, dv))


def peer_ffn(h, w_q, sub_k1, sub_k2, u_tab, v_tab):
    T = h.shape[0]
    q = (h @ w_q).reshape(T, PEER_HEADS, 2, PEER_HALF)
    s1 = jnp.einsum('thd,nd->thn', q[:, :, 0], sub_k1).astype(jnp.float32)
    s2 = jnp.einsum('thd,nd->thn', q[:, :, 1], sub_k2).astype(jnp.float32)
    v1, i1 = lax.top_k(s1, PEER_TOPK)
    v2, i2 = lax.top_k(s2, PEER_TOPK)
    cand = (v1[..., :, None] + v2[..., None, :]).reshape(T, PEER_HEADS, PEER_TOPK * PEER_TOPK)
    cidx = (i1[..., :, None] * N_KEYS + i2[..., None, :]).reshape(T, PEER_HEADS, PEER_TOPK * PEER_TOPK)
    top_v, top_c = lax.top_k(cand, PEER_TOPK)
    experts = jnp.take_along_axis(cidx, top_c, axis=-1).reshape(T, PEER_HEADS * PEER_TOPK)
    g = jax.nn.softmax(top_v, axis=-1).reshape(T, PEER_HEADS * PEER_TOPK).astype(h.dtype)
    n_blocks = -(-T // PEER_BLOCK)
    pad = n_blocks * PEER_BLOCK - T
    hp = jnp.pad(h, ((0, pad), (0, 0))).reshape(n_blocks, PEER_BLOCK, D_MODEL)
    ep = jnp.pad(experts, ((0, pad), (0, 0))).reshape(n_blocks, PEER_BLOCK, PEER_HEADS * PEER_TOPK)
    gp = jnp.pad(g, ((0, pad), (0, 0))).reshape(n_blocks, PEER_BLOCK, PEER_HEADS * PEER_TOPK)

    def one_block(args):
        hb, eb, gb = args
        act = jax.nn.gelu(jnp.einsum('td,ted->te', hb, u_tab[eb]), approximate=False)
        return jnp.einsum('te,ted->td', gb * act, v_tab[eb])

    out = lax.map(one_block, (hp, ep, gp)).reshape(n_blocks * PEER_BLOCK, D_MODEL)
    return out[:T]


def trunk_layer(x, c, pos, lw, lam_init, core):
    B, S = x.shape[0], x.shape[1]
    shift1, scale1, gate1, shift2, scale2, gate2 = adaln(c, lw['w_ada'], lw['b_ada'])
    h = layer_norm(x) * (1.0 + scale1) + shift1
    fq, fk, fv, fl, dq, dk, dv, ga, gb = jnp.split(h @ lw['w_in'], split_points(), axis=-1)
    fq = fq.reshape(B, S, FOX_HEADS, HEAD_DIM)
    fk = fk.reshape(B, S, FOX_HEADS, HEAD_DIM)
    fv = fv.reshape(B, S, FOX_HEADS, HEAD_DIM)
    logf = jax.nn.log_sigmoid(fl.astype(jnp.float32) + lw['b_f'].astype(jnp.float32))
    dq = rope(dq.reshape(B, S, 2 * DIFF_HEADS, HEAD_DIM), pos).reshape(B, S, DIFF_HEADS, 2, HEAD_DIM)
    dk = rope(dk.reshape(B, S, 2 * DIFF_HEADS, HEAD_DIM), pos).reshape(B, S, DIFF_HEADS, 2, HEAD_DIM)
    dv = dv.reshape(B, S, DIFF_HEADS, DIFF_V_DIM)
    lam = (jnp.exp(jnp.sum(lw['lq1'].astype(jnp.float32) * lw['lk1'].astype(jnp.float32)))
           - jnp.exp(jnp.sum(lw['lq2'].astype(jnp.float32) * lw['lk2'].astype(jnp.float32)))
           + lam_init)
    fo, do = core(fq, fk, fv, logf, dq, dk, dv, lam)
    do = rms_norm(do, lw['subln']) * (1.0 - lam_init)
    y_fox = fo.reshape(B, S, FOX_W) @ lw['w_o_fox']
    y_diff = do.reshape(B, S, DIFF_V_W) @ lw['w_o_diff']
    mix = (jax.nn.sigmoid(ga) * y_fox + jax.nn.sigmoid(gb) * y_diff) @ lw['w_out']
    x = layer_norm(DEEPNORM_ALPHA * x + gate1 * mix, lw['ln1_g'], lw['ln1_b'])
    h2 = layer_norm(x) * (1.0 + scale2) + shift2
    ffn = peer_ffn(h2.reshape(B * S, D_MODEL), lw['pq'], lw['pk1'], lw['pk2'],
                   lw['pu'], lw['pv']).reshape(B, S, D_MODEL)
    x = layer_norm(DEEPNORM_ALPHA * x + gate2 * ffn, lw['ln2_g'], lw['ln2_b'])
    return x, (fk, fv, logf, dk.reshape(B, S, DIFF_HEADS, 2 * HEAD_DIM), dv)


def setup_inputs(seed: int = 0) -> dict:
    key = jax.random.key(seed)
    ks = jax.random.split(key, 32)
    f32 = jnp.float32
    n_pages = PAST_LEN // PAGE_SIZE
    n_pool = (5 * DEC_BATCH * n_pages) // 4

    def nrm(k, shape, scale):
        return jax.random.normal(k, shape, f32) * scale

    gate_offset = jnp.repeat(jnp.array([0.0, 0.0, 1.0, 0.0, 0.0, 1.0], f32), D_MODEL)
    page_table = jax.random.permutation(ks[7], n_pool)[: DEC_BATCH * n_pages]
    return {
        'x_prompt': nrm(ks[0], (BATCH, SEQ, D_MODEL), 1.0),
        'x_sample': nrm(ks[1], (DEC_BATCH, DEC_SEQ, D_MODEL), 1.0),
        'cache_fox_k': nrm(ks[2], (DEPTH, n_pool, PAGE_SIZE, FOX_HEADS, HEAD_DIM), 1.0),
        'cache_fox_v': nrm(ks[3], (DEPTH, n_pool, PAGE_SIZE, FOX_HEADS, HEAD_DIM), 1.0),
        'cache_fox_logf': jax.nn.log_sigmoid(3.0 + nrm(ks[4], (DEPTH, n_pool, PAGE_SIZE, FOX_HEADS), 1.0)),
        'cache_diff_k': nrm(ks[5], (DEPTH, n_pool, PAGE_SIZE, DIFF_HEADS, 2 * HEAD_DIM), 1.0),
        'cache_diff_v': nrm(ks[6], (DEPTH, n_pool, PAGE_SIZE, DIFF_HEADS, DIFF_V_DIM), 1.0),
        'page_table': page_table.reshape(DEC_BATCH, n_pages).astype(jnp.int32),
        'c_prompt': nrm(ks[8], (BATCH, D_MODEL), 1.0),
        'c_sample': nrm(ks[9], (DEC_BATCH, D_MODEL), 1.0),
        'w_ada': nrm(ks[10], (DEPTH, D_MODEL, 6 * D_MODEL), 0.1 * D_MODEL ** -0.5),
        'b_ada': gate_offset + nrm(ks[11], (DEPTH, 6 * D_MODEL), 0.02),
        'w_in': nrm(ks[12], (DEPTH, D_MODEL, IN_W), D_MODEL ** -0.5),
        'b_f': 3.0 + nrm(ks[13], (DEPTH, FOX_HEADS), 0.5),
        'lambda_q1': nrm(ks[14], (DEPTH, HEAD_DIM), 0.1),
        'lambda_k1': nrm(ks[15], (DEPTH, HEAD_DIM), 0.1),
        'lambda_q2': nrm(ks[16], (DEPTH, HEAD_DIM), 0.1),
        'lambda_k2': nrm(ks[17], (DEPTH, HEAD_DIM), 0.1),
        'diff_subln_w': 1.0 + nrm(ks[18], (DEPTH, DIFF_V_DIM), 0.02),
        'w_o_fox': nrm(ks[19], (DEPTH, FOX_W, D_MODEL), FOX_W ** -0.5),
        'w_o_diff': nrm(ks[20], (DEPTH, DIFF_V_W, D_MODEL), DIFF_V_W ** -0.5),
        'w_out': nrm(ks[21], (DEPTH, D_MODEL, D_MODEL), DEEPNORM_BETA * D_MODEL ** -0.5),
        'ln1_g': 1.0 + nrm(ks[22], (DEPTH, D_MODEL), 0.02),
        'ln1_b': nrm(ks[23], (DEPTH, D_MODEL), 0.02),
        'peer_w_q': nrm(ks[24], (DEPTH, D_MODEL, PEER_HEADS * PEER_KEY_DIM), D_MODEL ** -0.5),
        'peer_sub_k1': nrm(ks[25], (DEPTH, N_KEYS, PEER_HALF), PEER_HALF ** -0.5),
        'peer_sub_k2': nrm(ks[26], (DEPTH, N_KEYS, PEER_HALF), PEER_HALF ** -0.5),
        'peer_u': nrm(ks[27], (DEPTH, N_EXPERTS, D_MODEL), D_MODEL ** -0.5),
        'peer_v': nrm(ks[28], (DEPTH, N_EXPERTS, D_MODEL), DEEPNORM_BETA * PEER_HEADS ** -0.5),
        'ln2_g': 1.0 + nrm(ks[29], (DEPTH, D_MODEL), 0.02),
        'ln2_b': nrm(ks[30], (DEPTH, D_MODEL), 0.02),
    }


def reference(x_prompt, x_sample, cache_fox_k, cache_fox_v, cache_fox_logf, cache_diff_k,
              cache_diff_v, page_table, c_prompt, c_sample, w_ada, b_ada, w_in, b_f,
              lambda_q1, lambda_k1, lambda_q2, lambda_k2, diff_subln_w, w_o_fox, w_o_diff,
              w_out, ln1_g, ln1_b, peer_w_q, peer_sub_k1, peer_sub_k2, peer_u, peer_v,
              ln2_g, ln2_b):
    n_past = page_table.shape[1] * PAGE_SIZE
    pos_prompt = jnp.arange(x_prompt.shape[1])
    pos_sample = n_past + jnp.arange(x_sample.shape[1])
    xp, xs = x_prompt, x_sample
    states_p, states_s = [], []
    for l in range(DEPTH):
        lw = {
            'w_ada': w_ada[l], 'b_ada': b_ada[l], 'w_in': w_in[l], 'b_f': b_f[l],
            'lq1': lambda_q1[l], 'lk1': lambda_k1[l], 'lq2': lambda_q2[l], 'lk2': lambda_k2[l],
            'subln': diff_subln_w[l], 'w_o_fox': w_o_fox[l], 'w_o_diff': w_o_diff[l],
            'w_out': w_out[l], 'ln1_g': ln1_g[l], 'ln1_b': ln1_b[l],
            'pq': peer_w_q[l], 'pk1': peer_sub_k1[l], 'pk2': peer_sub_k2[l],
            'pu': peer_u[l], 'pv': peer_v[l], 'ln2_g': ln2_g[l], 'ln2_b': ln2_b[l],
        }
        lam_init = 0.8 - 0.6 * math.exp(-0.3 * l)
        xp, sp = trunk_layer(xp, c_prompt, pos_prompt, lw, lam_init, prompt_core)
        core_s = functools.partial(sample_core, l, cache_fox_k, cache_fox_v, cache_fox_logf,
                                   cache_diff_k, cache_diff_v, page_table)
        xs, ss = trunk_layer(xs, c_sample, pos_sample, lw, lam_init, core_s)
        states_p.append(sp)
        states_s.append(ss)
    fk_p = jnp.stack([s[0] for s in states_p])
    fv_p = jnp.stack([s[1] for s in states_p])
    fl_p = jnp.stack([s[2] for s in states_p])
    dk_p = jnp.stack([s[3] for s in states_p])
    dv_p = jnp.stack([s[4] for s in states_p])
    fk_s = jnp.stack([s[0] for s in states_s])
    fv_s = jnp.stack([s[1] for s in states_s])
    fl_s = jnp.stack([s[2] for s in states_s])
    dk_s = jnp.stack([s[3] for s in states_s])
    dv_s = jnp.stack([s[4] for s in states_s])
    return (xp, xs, fk_p, fv_p, fl_p, dk_p, dv_p, fk_s, fv_s, fl_s, dk_s, dv_s)
```

```python
import functools
import math

import jax
import jax.numpy as jnp
from jax import lax
from jax.experimental import pallas as pl
from jax.experimental.pallas import tpu as pltpu

F32 = jnp.float32
BF16 = jnp.bfloat16
NEG_INF = float("-inf")

LN_EPS = 1e-5
ROPE_THETA = 10000.0
HEAD_DIM = 128
FOX_HEADS = 8
DIFF_HEADS = 4
N_MAPS = 8
FOX_W = FOX_HEADS * HEAD_DIM
PEER_HEADS = 8
PEER_TOPK = 16
N_KEYS = 128
ATTN_SCALE = HEAD_DIM ** -0.5
LANES = 128
VMEM_LIMIT = 48 * 1024 * 1024

SDS = jax.ShapeDtypeStruct


def _cp(*sem):
    return pltpu.CompilerParams(dimension_semantics=sem, vmem_limit_bytes=VMEM_LIMIT)


def _ln(x):
    mu = jnp.mean(x, axis=-1, keepdims=True)
    xc = x - mu
    var = jnp.mean(xc * xc, axis=-1, keepdims=True)
    return xc * lax.rsqrt(var + LN_EPS)


def _dot_nt(a, b):
    return lax.dot_general(a, b, (((1,), (1,)), ((), ())), preferred_element_type=F32)


def _ada_kernel(c_ref, w_ref, b_ref, o_ref):
    c = c_ref[...]
    a = (c * jax.nn.sigmoid(c)).astype(BF16)
    o_ref[...] = jnp.dot(a, w_ref[...].astype(BF16), preferred_element_type=F32) + b_ref[...]


def _ada_call(c_all, w_ada, b_ada, layer):
    mp, d = c_all.shape
    n = w_ada.shape[2]
    tn = 1024
    return pl.pallas_call(
        _ada_kernel,
        grid=(n // tn,),
        in_specs=[
            pl.BlockSpec((mp, d), lambda j: (0, 0)),
            pl.BlockSpec((None, d, tn), lambda j: (layer, 0, j)),
            pl.BlockSpec((None, 1, tn), lambda j: (layer, 0, j)),
        ],
        out_specs=pl.BlockSpec((mp, tn), lambda j: (0, j)),
        out_shape=SDS((mp, n), F32),
        compiler_params=_cp("arbitrary"),
        name="ada",
    )(c_all, w_ada, b_ada.reshape(b_ada.shape[0], 1, n))


def _lnmod_mm_kernel(x_ref, sc_ref, sh_ref, w_ref, o_ref, *rest, emit_ht):
    h_scr = rest[-1]

    @pl.when(pl.program_id(1) == 0)
    def _():
        h = _ln(x_ref[...]) * (1.0 + sc_ref[...]) + sh_ref[...]
        h_scr[...] = h.astype(BF16)
        if emit_ht:
            rest[0][...] = h.T.astype(BF16)

    o_ref[...] = jnp.dot(h_scr[...], w_ref[...], preferred_element_type=F32)


def _lnmod_mm_call(x, sc, sh, w, tm, tn, emit_ht=False):
    t, d = x.shape
    n = w.shape[1]
    r = sc.shape[0]
    mod_spec = pl.BlockSpec((1, d), lambda i, j: (0, 0)) if r == 1 else pl.BlockSpec((tm, d), lambda i, j: (i, 0))
    out_specs = [pl.BlockSpec((tm, tn), lambda i, j: (i, j))]
    out_shape = [SDS((t, n), F32)]
    if emit_ht:
        out_specs.append(pl.BlockSpec((d, tm), lambda i, j: (0, i)))
        out_shape.append(SDS((d, t), BF16))
    res = pl.pallas_call(
        functools.partial(_lnmod_mm_kernel, emit_ht=emit_ht),
        grid=(t // tm, n // tn),
        in_specs=[
            pl.BlockSpec((tm, d), lambda i, j: (i, 0)),
            mod_spec,
            mod_spec,
            pl.BlockSpec((d, tn), lambda i, j: (0, j)),
        ],
        out_specs=out_specs,
        out_shape=out_shape,
        scratch_shapes=[pltpu.VMEM((tm, d), BF16)],
        compiler_params=_cp("parallel", "arbitrary"),
        name="lnmod_mm",
    )(x, sc, sh, w)
    return res if emit_ht else res[0]


def _post_kernel(dq_ref, dk_ref, fl_ref, cos_ref, sin_ref, bf_ref, dqo_ref, dko_ref, lf_ref, *rest, tm, with_cum):
    c = cos_ref[...]
    s = sin_ref[...]
    for j in range(N_MAPS):
        sl = slice(j * HEAD_DIM, (j + 1) * HEAD_DIM)
        x = dq_ref[:, sl]
        dqo_ref[:, sl] = x * c + pltpu.roll(x, HEAD_DIM // 2, axis=1) * s
        y = dk_ref[:, sl]
        dko_ref[:, sl] = y * c + pltpu.roll(y, HEAD_DIM // 2, axis=1) * s
    z = fl_ref[...] + bf_ref[...]
    lf = jnp.minimum(z, 0.0) - jnp.log1p(jnp.exp(-jnp.abs(z)))
    lf_ref[...] = lf[:, :FOX_HEADS]
    if with_cum:
        cum_ref, cumt_ref, carry_scr = rest

        @pl.when(pl.program_id(0) == 0)
        def _():
            carry_scr[...] = jnp.zeros_like(carry_scr)

        row = lax.broadcasted_iota(jnp.int32, (tm, tm), 0)
        col = lax.broadcasted_iota(jnp.int32, (tm, tm), 1)
        tri = jnp.where(col <= row, 1.0, 0.0).astype(BF16)
        hi = lf.astype(BF16)
        r1 = lf - hi.astype(F32)
        mid = r1.astype(BF16)
        lo = (r1 - mid.astype(F32)).astype(BF16)
        cum = (jnp.dot(tri, hi, preferred_element_type=F32)
               + jnp.dot(tri, mid, preferred_element_type=F32)
               + jnp.dot(tri, lo, preferred_element_type=F32)) + carry_scr[...]
        carry_scr[...] = cum[tm - 1:tm, :]
        cum_ref[...] = cum[:, :FOX_HEADS]
        cumt_ref[...] = cum.T[:FOX_HEADS, :]


def _post_call(proj, cos2, sin2, bf_pad, tm, with_cum):
    t = proj.shape[0]
    r = cos2.shape[0]
    tab_spec = pl.BlockSpec((1, LANES), lambda i: (0, 0)) if r == 1 else pl.BlockSpec((tm, LANES), lambda i: (i, 0))
    out_specs = [
        pl.BlockSpec((tm, FOX_W), lambda i: (i, 0)),
        pl.BlockSpec((tm, FOX_W), lambda i: (i, 0)),
        pl.BlockSpec((tm, FOX_HEADS), lambda i: (i, 0)),
    ]
    out_shape = [SDS((t, FOX_W), F32), SDS((t, FOX_W), F32), SDS((t, FOX_HEADS), F32)]
    scratch = []
    if with_cum:
        out_specs += [pl.BlockSpec((tm, FOX_HEADS), lambda i: (i, 0)), pl.BlockSpec((FOX_HEADS, tm), lambda i: (0, i))]
        out_shape += [SDS((t, FOX_HEADS), F32), SDS((FOX_HEADS, t), F32)]
        scratch = [pltpu.VMEM((1, LANES), F32)]
    return pl.pallas_call(
        functools.partial(_post_kernel, tm=tm, with_cum=with_cum),
        grid=(t // tm,),
        in_specs=[
            pl.BlockSpec((tm, FOX_W), lambda i: (i, 3)),
            pl.BlockSpec((tm, FOX_W), lambda i: (i, 4)),
            pl.BlockSpec((tm, LANES), lambda i: (i, 80)),
            tab_spec,
            tab_spec,
            pl.BlockSpec((1, LANES), lambda i: (0, 0)),
        ],
        out_specs=out_specs,
        out_shape=out_shape,
        scratch_shapes=scratch,
        compiler_params=_cp("arbitrary"),
        name="post_proj",
    )(proj, proj, proj, cos2, sin2, bf_pad)


def _online_update(s, v_bf, m_ref, l_ref, acc_ref, idx, acc_sl):
    m_prev = m_ref[idx]
    m_new = jnp.maximum(m_prev, jnp.max(s, axis=1, keepdims=True))
    alpha = jnp.exp(m_prev - m_new)
    p = jnp.exp(s - m_new)
    l_ref[idx] = alpha * l_ref[idx] + jnp.sum(p, axis=1, keepdims=True)
    acc_ref[:, acc_sl] = alpha * acc_ref[:, acc_sl] + jnp.dot(p.astype(BF16), v_bf, preferred_element_type=F32)
    m_ref[idx] = m_new


def _flash_init(m_scr, l_scr, acc_scr):
    m_scr[...] = jnp.full_like(m_scr, NEG_INF)
    l_scr[...] = jnp.zeros_like(l_scr)
    acc_scr[...] = jnp.zeros_like(acc_scr)


def _causal_keep(qi, ki, tq, tk):
    row = lax.broadcasted_iota(jnp.int32, (tq, tk), 0)
    col = lax.broadcasted_iota(jnp.int32, (tq, tk), 1)
    return jnp.logical_or(col <= row, ki < qi)


def _fox_flash_kernel(q_ref, k_ref, v_ref, cq_ref, ck_ref, o_ref, m_scr, l_scr, acc_scr, *, tq, tk):
    qi = pl.program_id(0)
    ki = pl.program_id(1)

    @pl.when(ki == 0)
    def _():
        _flash_init(m_scr, l_scr, acc_scr)

    @pl.when(ki <= qi)
    def _():
        keep = _causal_keep(qi, ki, tq, tk)
        for h in range(FOX_HEADS):
            sl = slice(h * HEAD_DIM, (h + 1) * HEAD_DIM)
            q = (q_ref[:, sl] * ATTN_SCALE).astype(BF16)
            s = _dot_nt(q, k_ref[:, sl].astype(BF16))
            s = s + (cq_ref[:, h:h + 1] - ck_ref[h:h + 1, :])
            s = jnp.where(keep, s, NEG_INF)
            _online_update(s, v_ref[:, sl].astype(BF16), m_scr, l_scr, acc_scr, h, sl)

    @pl.when(ki == qi)
    def _():
        for h in range(FOX_HEADS):
            sl = slice(h * HEAD_DIM, (h + 1) * HEAD_DIM)
            o_ref[:, sl] = (acc_scr[:, sl] / l_scr[h]).astype(o_ref.dtype)


def _fox_flash_call(proj, cum, cumt, tq):
    t = proj.shape[0]
    nq = t // tq
    kv_idx = lambda c: (lambda qi, ki: (jnp.minimum(ki, qi), c))
    return pl.pallas_call(
        functools.partial(_fox_flash_kernel, tq=tq, tk=tq),
        grid=(nq, nq),
        in_specs=[
            pl.BlockSpec((tq, FOX_W), lambda qi, ki: (qi, 0)),
            pl.BlockSpec((tq, FOX_W), kv_idx(1)),
            pl.BlockSpec((tq, FOX_W), kv_idx(2)),
            pl.BlockSpec((tq, FOX_HEADS), lambda qi, ki: (qi, 0)),
            pl.BlockSpec((FOX_HEADS, tq), lambda qi, ki: (0, jnp.minimum(ki, qi))),
        ],
        out_specs=pl.BlockSpec((tq, FOX_W), lambda qi, ki: (qi, 0)),
        out_shape=SDS((t, FOX_W), BF16),
        scratch_shapes=[
            pltpu.VMEM((N_MAPS, tq, 1), F32),
            pltpu.VMEM((N_MAPS, tq, 1), F32),
            pltpu.VMEM((tq, FOX_W), F32),
        ],
        compiler_params=_cp("parallel", "arbitrary"),
        name="fox_flash",
    )(proj, proj, proj, cum, cumt)


def _lambda_full(lq1_ref, lk1_ref, lq2_ref, lk2_ref, lam_init):
    a = jnp.exp(jnp.sum(lq1_ref[...] * lk1_ref[...], axis=1, keepdims=True))
    b = jnp.exp(jnp.sum(lq2_ref[...] * lk2_ref[...], axis=1, keepdims=True))
    return a - b + lam_init


def _diff_flash_kernel(q_ref, k_ref, v_ref, lq1_ref, lk1_ref, lq2_ref, lk2_ref, sub_ref,
                       o_ref, m_scr, l_scr, acc_scr, *, tq, tk, lam_init):
    qi = pl.program_id(0)
    ki = pl.program_id(1)
    dv = 2 * HEAD_DIM

    @pl.when(ki == 0)
    def _():
        _flash_init(m_scr, l_scr, acc_scr)

    @pl.when(ki <= qi)
    def _():
        keep = _causal_keep(qi, ki, tq, tk)
        for hd in range(DIFF_HEADS):
            v_bf = v_ref[:, hd * dv:(hd + 1) * dv].astype(BF16)
            for n in range(2):
                mp = 2 * hd + n
                sl = slice(mp * HEAD_DIM, (mp + 1) * HEAD_DIM)
                q = (q_ref[:, sl] * ATTN_SCALE).astype(BF16)
                s = _dot_nt(q, k_ref[:, sl].astype(BF16))
                s = jnp.where(keep, s, NEG_INF)
                _online_update(s, v_bf, m_scr, l_scr, acc_scr, mp, slice(mp * dv, (mp + 1) * dv))

    @pl.when(ki == qi)
    def _():
        lam = _lambda_full(lq1_ref, lk1_ref, lq2_ref, lk2_ref, lam_init)
        for hd in range(DIFF_HEADS):
            o1 = acc_scr[:, (2 * hd) * dv:(2 * hd + 1) * dv] / l_scr[2 * hd]
            o2 = acc_scr[:, (2 * hd + 1) * dv:(2 * hd + 2) * dv] / l_scr[2 * hd + 1]
            d = o1 - lam * o2
            y = d * lax.rsqrt(jnp.mean(d * d, axis=1, keepdims=True) + LN_EPS)
            o_ref[:, hd * dv:(hd + 1) * dv] = ((y * sub_ref[...]) * (1.0 - lam_init)).astype(o_ref.dtype)


def _diff_flash_call(dq_r, dk_r, proj, lam_vecs, subln, tq, lam_init):
    t = proj.shape[0]
    nq = t // tq
    vec = pl.BlockSpec((1, HEAD_DIM), lambda qi, ki: (0, 0))
    return pl.pallas_call(
        functools.partial(_diff_flash_kernel, tq=tq, tk=tq, lam_init=lam_init),
        grid=(nq, nq),
        in_specs=[
            pl.BlockSpec((tq, FOX_W), lambda qi, ki: (qi, 0)),
            pl.BlockSpec((tq, FOX_W), lambda qi, ki: (jnp.minimum(ki, qi), 0)),
            pl.BlockSpec((tq, FOX_W), lambda qi, ki: (jnp.minimum(ki, qi), 5)),
            vec, vec, vec, vec,
            pl.BlockSpec((1, 2 * HEAD_DIM), lambda qi, ki: (0, 0)),
        ],
        out_specs=pl.BlockSpec((tq, FOX_W), lambda qi, ki: (qi, 0)),
        out_shape=SDS((t, FOX_W), BF16),
        scratch_shapes=[
            pltpu.VMEM((N_MAPS, tq, 1), F32),
            pltpu.VMEM((N_MAPS, tq, 1), F32),
            pltpu.VMEM((tq, N_MAPS * 2 * HEAD_DIM), F32),
        ],
        compiler_params=_cp("parallel", "arbitrary"),
        name="diff_flash",
    )(dq_r, dk_r, proj, *lam_vecs, subln)


def _class_allreduce(x, op, period):
    s = period
    while s < LANES:
        x = op(x, pltpu.roll(x, s, axis=1))
        s *= 2
    return x


def _fold_chunks(x, op):
    out = x[:, :LANES]
    for c in range(1, x.shape[1] // LANES):
        out = op(out, x[:, c * LANES:(c + 1) * LANES])
    return out


def _rep_chunks(x, k):
    return jnp.concatenate([x] * k, axis=1)


def _suffix_incl(x, period):
    n = x.shape[1]
    lane = lax.broadcasted_iota(jnp.int32, x.shape, 1)
    s = period
    while s < n:
        y = pltpu.roll(x, n - s, axis=1)
        x = x + jnp.where(lane < n - s, y, 0.0)
        s *= 2
    return x


def _fox_dec_kernel(pt_ref, q_ref, kn_ref, vn_ref, lfn_ref, *refs, pg, nj):
    del pt_ref
    k_refs, v_refs, lf_refs = refs[:pg], refs[pg:2 * pg], refs[2 * pg:3 * pg]
    o_ref = refs[3 * pg]
    m_scr, l_scr, acc_scr, carry_scr = refs[3 * pg + 1:]
    j = pl.program_id(1)
    nh = FOX_HEADS
    w = PAGE_ROWS_FOX
    sub = lax.broadcasted_iota(jnp.int32, (nh, w), 0)
    lane = lax.broadcasted_iota(jnp.int32, (nh, w), 1)
    diag = sub == (lane & (nh - 1))
    diag1 = diag[:, :LANES]

    def to_sub(x_cls):
        return jnp.max(jnp.where(diag1, jnp.broadcast_to(x_cls, (nh, LANES)), NEG_INF), axis=1, keepdims=True)

    @pl.when(j == 0)
    def _():
        m_scr[...] = jnp.full_like(m_scr, NEG_INF)
        l_scr[...] = jnp.zeros_like(l_scr)
        acc_scr[...] = jnp.zeros_like(acc_scr)
        carry_scr[...] = lfn_ref[...]

    qb = (q_ref[...] * ATTN_SCALE).astype(BF16)

    lf = jnp.concatenate([r[...] for r in lf_refs], axis=0)
    tot = _class_allreduce(_fold_chunks(lf, jnp.add), jnp.add, nh)
    run = carry_scr[...]
    before = []
    for i in range(pg):
        before.append(run)
        run = run + tot[i:i + 1, :]
    carry_scr[...] = run
    bias = (_suffix_incl(lf, nh) - lf) + _rep_chunks(jnp.concatenate(before, axis=0), w // LANES)

    rows = []
    for i in range(pg):
        k2 = k_refs[i][...].reshape(w, HEAD_DIM).astype(BF16)
        st = _dot_nt(qb, k2)
        rows.append(jnp.sum(jnp.where(diag, st, 0.0), axis=0, keepdims=True))
    s = jnp.concatenate(rows, axis=0) + bias

    mc = _class_allreduce(_fold_chunks(jnp.max(s, axis=0, keepdims=True), jnp.maximum), jnp.maximum, nh)
    m_old = m_scr[...]
    m_new = jnp.maximum(m_old, mc)
    alpha = jnp.exp(m_old - m_new)
    p = jnp.exp(s - _rep_chunks(m_new, w // LANES))
    l_scr[...] = l_scr[...] * _rep_chunks(alpha, w // LANES) + p

    acc_c = jnp.zeros((nh, HEAD_DIM), F32)
    for i in range(pg):
        pm = jnp.where(diag, jnp.broadcast_to(p[i:i + 1, :], (nh, w)), 0.0).astype(BF16)
        v2 = v_refs[i][...].reshape(w, HEAD_DIM).astype(BF16)
        acc_c = acc_c + jnp.dot(pm, v2, preferred_element_type=F32)
    acc_scr[...] = acc_scr[...] * to_sub(alpha) + acc_c
    m_scr[...] = m_new

    @pl.when(j == nj - 1)
    def _():
        kn = kn_ref[...].astype(BF16).astype(F32)
        s_self = jnp.sum(qb.astype(F32) * kn, axis=1, keepdims=True)
        s_cls = jnp.sum(jnp.where(diag1, jnp.broadcast_to(s_self, (nh, LANES)), 0.0), axis=0, keepdims=True)
        m_fin = jnp.maximum(m_new, s_cls)
        a2 = jnp.exp(m_new - m_fin)
        p_self = jnp.exp(s_cls - m_fin)
        l_cls = _class_allreduce(_fold_chunks(jnp.sum(l_scr[...], axis=0, keepdims=True), jnp.add), jnp.add, nh)
        l_cls = l_cls * a2 + p_self
        o = (acc_scr[...] * to_sub(a2) + to_sub(p_self) * vn_ref[...]) / to_sub(l_cls)
        o_ref[...] = o


PAGE_ROWS_FOX = 128 * FOX_HEADS
PAGE_ROWS_DIFF = 128 * DIFF_HEADS


def _fox_dec_call(layer, page_table, q, k_new, v_new, lfn_cls, cache_k, cache_v, cache_lf_flat, pg):
    b = q.shape[0]
    n_pages = page_table.shape[1]
    page = cache_k.shape[2]
    nj = n_pages // pg

    def page_idx(i):
        return lambda bb, j, pt: (layer, pt[bb, n_pages - 1 - (j * pg + i)], 0, 0, 0)

    def lf_idx(i):
        return lambda bb, j, pt: (layer, pt[bb, n_pages - 1 - (j * pg + i)], 0, 0)

    tok = pl.BlockSpec((None, FOX_HEADS, HEAD_DIM), lambda bb, j, pt: (bb, 0, 0))
    in_specs = [tok, tok, tok, pl.BlockSpec((None, 1, LANES), lambda bb, j, pt: (bb, 0, 0))]
    in_specs += [pl.BlockSpec((None, None, page, FOX_HEADS, HEAD_DIM), page_idx(i)) for i in range(pg)]
    in_specs += [pl.BlockSpec((None, None, page, FOX_HEADS, HEAD_DIM), page_idx(i)) for i in range(pg)]
    in_specs += [pl.BlockSpec((None, None, 1, page * FOX_HEADS), lf_idx(i)) for i in range(pg)]
    grid_spec = pltpu.PrefetchScalarGridSpec(
        num_scalar_prefetch=1,
        grid=(b, nj),
        in_specs=in_specs,
        out_specs=pl.BlockSpec((None, FOX_HEADS, HEAD_DIM), lambda bb, j, pt: (bb, 0, 0)),
        scratch_shapes=[
            pltpu.VMEM((1, LANES), F32),
            pltpu.VMEM((pg, page * FOX_HEADS), F32),
            pltpu.VMEM((FOX_HEADS, HEAD_DIM), F32),
            pltpu.VMEM((1, LANES), F32),
        ],
    )
    return pl.pallas_call(
        functools.partial(_fox_dec_kernel, pg=pg, nj=nj),
        grid_spec=grid_spec,
        out_shape=SDS((b, FOX_HEADS, HEAD_DIM), F32),
        compiler_params=_cp("parallel", "arbitrary"),
        name="fox_decode",
    )(page_table, q, k_new, v_new, lfn_cls, *([cache_k] * pg), *([cache_v] * pg), *([cache_lf_flat] * pg))


def _diff_dec_kernel(pt_ref, q_ref, kn_ref, vn_ref, lq1_ref, lk1_ref, lq2_ref, lk2_ref, sub_ref, *refs,
                     pg, nj, lam_init):
    del pt_ref
    k_refs, v_refs = refs[:pg], refs[pg:2 * pg]
    o_ref = refs[2 * pg]
    m_scr, l_scr, acc_scr = refs[2 * pg + 1:]
    j = pl.program_id(1)
    nh = DIFF_HEADS
    w = PAGE_ROWS_DIFF
    dv = 2 * HEAD_DIM
    sub = lax.broadcasted_iota(jnp.int32, (N_MAPS, w), 0)
    lane = lax.broadcasted_iota(jnp.int32, (N_MAPS, w), 1)
    same_head = (sub >> 1) == (lane & (nh - 1))
    sel = [jnp.logical_and(same_head, (sub & 1) == n) for n in range(2)]
    sel1 = [x[:, :LANES] for x in sel]

    def to_sub(x0, x1):
        y = jnp.where(sel1[0], jnp.broadcast_to(x0, (N_MAPS, LANES)),
                      jnp.where(sel1[1], jnp.broadcast_to(x1, (N_MAPS, LANES)), NEG_INF))
        return jnp.max(y, axis=1, keepdims=True)

    @pl.when(j == 0)
    def _():
        m_scr[...] = jnp.full_like(m_scr, NEG_INF)
        l_scr[...] = jnp.zeros_like(l_scr)
        acc_scr[...] = jnp.zeros_like(acc_scr)

    q8 = q_ref[...] * ATTN_SCALE
    odd = (lax.broadcasted_iota(jnp.int32, (N_MAPS, HEAD_DIM), 0) & 1) == 1
    qd = jnp.concatenate([jnp.where(odd, 0.0, q8), jnp.where(odd, q8, 0.0)], axis=1).astype(BF16)

    rows = [[], []]
    for i in range(pg):
        k2 = k_refs[i][...].reshape(w, dv).astype(BF16)
        st = _dot_nt(qd, k2)
        for n in range(2):
            rows[n].append(jnp.sum(jnp.where(sel[n], st, 0.0), axis=0, keepdims=True))
    s = [jnp.concatenate(rows[n], axis=0) for n in range(2)]

    p, alpha = [], []
    for n in range(2):
        mc = _class_allreduce(_fold_chunks(jnp.max(s[n], axis=0, keepdims=True), jnp.maximum), jnp.maximum, nh)
        m_old = m_scr[n]
        m_new = jnp.maximum(m_old, mc)
        a = jnp.exp(m_old - m_new)
        pn = jnp.exp(s[n] - _rep_chunks(m_new, w // LANES))
        l_scr[n] = l_scr[n] * _rep_chunks(a, w // LANES) + pn
        m_scr[n] = m_new
        p.append(pn)
        alpha.append(a)

    acc_c = jnp.zeros((N_MAPS, dv), F32)
    for i in range(pg):
        pm = jnp.where(sel[0], jnp.broadcast_to(p[0][i:i + 1, :], (N_MAPS, w)),
                       jnp.where(sel[1], jnp.broadcast_to(p[1][i:i + 1, :], (N_MAPS, w)), 0.0)).astype(BF16)
        v2 = v_refs[i][...].reshape(w, dv).astype(BF16)
        acc_c = acc_c + jnp.dot(pm, v2, preferred_element_type=F32)
    acc_scr[...] = acc_scr[...] * to_sub(alpha[0], alpha[1]) + acc_c

    @pl.when(j == nj - 1)
    def _():
        kn = kn_ref[...].astype(BF16).astype(F32)
        s_self = jnp.sum(q8.astype(BF16).astype(F32) * kn, axis=1, keepdims=True)
        a2, p_self, l_cls = [], [], []
        for n in range(2):
            s_cls = jnp.sum(jnp.where(sel1[n], jnp.broadcast_to(s_self, (N_MAPS, LANES)), 0.0),
                            axis=0, keepdims=True)
            m_new = m_scr[n]
            m_fin = jnp.maximum(m_new, s_cls)
            a = jnp.exp(m_new - m_fin)
            ps = jnp.exp(s_cls - m_fin)
            lc = _class_allreduce(_fold_chunks(jnp.sum(l_scr[n], axis=0, keepdims=True), jnp.add), jnp.add, nh)
            a2.append(a)
            p_self.append(ps)
            l_cls.append(lc * a + ps)
        o = (acc_scr[...] * to_sub(a2[0], a2[1]) + to_sub(p_self[0], p_self[1]) * vn_ref[...]) / to_sub(l_cls[0], l_cls[1])
        lam = _lambda_full(lq1_ref, lk1_ref, lq2_ref, lk2_ref, lam_init)
        d = o - lam * pltpu.roll(o, N_MAPS - 1, axis=0)
        y = d * lax.rsqrt(jnp.mean(d * d, axis=1, keepdims=True) + LN_EPS)
        o_ref[...] = (y * sub_ref[...]) * (1.0 - lam_init)


def _diff_dec_call(layer, page_table, q8, k_new8, v_new8, lam_vecs, subln, cache_k, cache_v, pg, lam_init):
    b = q8.shape[0]
    n_pages = page_table.shape[1]
    page = cache_k.shape[2]
    nj = n_pages // pg
    dv = 2 * HEAD_DIM

    def page_idx(i):
        return lambda bb, j, pt: (layer, pt[bb, n_pages - 1 - (j * pg + i)], 0, 0, 0)

    vec = pl.BlockSpec((1, HEAD_DIM), lambda bb, j, pt: (0, 0))
    tok = pl.BlockSpec((None, N_MAPS, HEAD_DIM), lambda bb, j, pt: (bb, 0, 0))
    in_specs = [tok, tok, pl.BlockSpec((None, N_MAPS, dv), lambda bb, j, pt: (bb, 0, 0)),
                vec, vec, vec, vec, pl.BlockSpec((1, dv), lambda bb, j, pt: (0, 0))]
    in_specs += [pl.BlockSpec((None, None, page, DIFF_HEADS, dv), page_idx(i)) for i in range(pg)]
    in_specs += [pl.BlockSpec((None, None, page, DIFF_HEADS, dv), page_idx(i)) for i in range(pg)]
    grid_spec = pltpu.PrefetchScalarGridSpec(
        num_scalar_prefetch=1,
        grid=(b, nj),
        in_specs=in_specs,
        out_specs=pl.BlockSpec((None, N_MAPS, dv), lambda bb, j, pt: (bb, 0, 0)),
        scratch_shapes=[
            pltpu.VMEM((2, 1, LANES), F32),
            pltpu.VMEM((2, pg, page * DIFF_HEADS), F32),
            pltpu.VMEM((N_MAPS, dv), F32),
        ],
    )
    return pl.pallas_call(
        functools.partial(_diff_dec_kernel, pg=pg, nj=nj, lam_init=lam_init),
        grid_spec=grid_spec,
        out_shape=SDS((b, N_MAPS, dv), F32),
        compiler_params=_cp("parallel", "arbitrary"),
        name="diff_decode",
    )(page_table, q8, k_new8, v_new8, *lam_vecs, subln, *([cache_k] * pg), *([cache_v] * pg))


def _mix_kernel(fo_ref, dn_ref, ga_ref, gb_ref, wf_ref, wd_ref, o_ref):
    yf = jnp.dot(fo_ref[...].astype(BF16), wf_ref[...], preferred_element_type=F32)
    yd = jnp.dot(dn_ref[...].astype(BF16), wd_ref[...], preferred_element_type=F32)
    o_ref[...] = (jax.nn.sigmoid(ga_ref[...]) * yf + jax.nn.sigmoid(gb_ref[...]) * yd).astype(o_ref.dtype)


def _mix_call(fo, dn, proj, wf, wd, tm):
    t = fo.shape[0]
    d = wf.shape[1]
    tn = 1024
    nb = d // tn
    return pl.pallas_call(
        _mix_kernel,
        grid=(t // tm, nb),
        in_specs=[
            pl.BlockSpec((tm, FOX_W), lambda i, j: (i, 0)),
            pl.BlockSpec((tm, FOX_W), lambda i, j: (i, 0)),
            pl.BlockSpec((tm, tn), lambda i, j: (i, 3 * nb + j)),
            pl.BlockSpec((tm, tn), lambda i, j: (i, 4 * nb + j)),
            pl.BlockSpec((FOX_W, tn), lambda i, j: (0, j)),
            pl.BlockSpec((FOX_W, tn), lambda i, j: (0, j)),
        ],
        out_specs=pl.BlockSpec((tm, tn), lambda i, j: (i, j)),
        out_shape=SDS((t, d), BF16),
        compiler_params=_cp("parallel", "arbitrary"),
        name="mix",
    )(fo, dn, proj, proj, wf, wd)


def _resid_ln_kernel(x_ref, y_ref, gate_ref, w_ref, g_ref, b_ref, o_ref, *, alpha):
    mix = jnp.dot(y_ref[...], w_ref[...], preferred_element_type=F32)
    z = alpha * x_ref[...] + gate_ref[...] * mix
    o_ref[...] = _ln(z) * g_ref[...] + b_ref[...]


def _mod_spec(r, tm, d):
    return pl.BlockSpec((1, d), lambda i: (0, 0)) if r == 1 else pl.BlockSpec((tm, d), lambda i: (i, 0))


def _resid_ln_call(x, y, gate, w, g, b, tm, alpha):
    t, d = x.shape
    row = pl.BlockSpec((tm, d), lambda i: (i, 0))
    vec = pl.BlockSpec((1, d), lambda i: (0, 0))
    return pl.pallas_call(
        functools.partial(_resid_ln_kernel, alpha=alpha),
        grid=(t // tm,),
        in_specs=[row, row, _mod_spec(gate.shape[0], tm, d), pl.BlockSpec((d, d), lambda i: (0, 0)), vec, vec],
        out_specs=row,
        out_shape=SDS((t, d), F32),
        compiler_params=_cp("parallel"),
        name="resid_ln1",
    )(x, y, gate, w, g, b)


def _final_ln_kernel(x_ref, ft_ref, gate_ref, g_ref, b_ref, o_ref, *, alpha):
    z = alpha * x_ref[...] + gate_ref[...] * ft_ref[...].T
    o_ref[...] = _ln(z) * g_ref[...] + b_ref[...]


def _final_ln_call(x, ffn_t, gate, g, b, tm, alpha):
    t, d = x.shape
    row = pl.BlockSpec((tm, d), lambda i: (i, 0))
    vec = pl.BlockSpec((1, d), lambda i: (0, 0))
    return pl.pallas_call(
        functools.partial(_final_ln_kernel, alpha=alpha),
        grid=(t // tm,),
        in_specs=[row, pl.BlockSpec((d, tm), lambda i: (0, i)), _mod_spec(gate.shape[0], tm, d), vec, vec],
        out_specs=row,
        out_shape=SDS((t, d), F32),
        compiler_params=_cp("parallel"),
        name="final_ln2",
    )(x, ffn_t, gate, g, b)


def _top_values(s, k):
    cur = s
    vals = []
    for _ in range(k):
        m = jnp.max(cur, axis=1, keepdims=True)
        vals.append(m)
        cur = jnp.where(cur == m, NEG_INF, cur)
    return vals


def _peer_sel_kernel(qp_ref, k1_ref, k2_ref, s1t_ref, s2t_ref, e1t_ref, e2t_ref, tau_ref, *, tm):
    half = N_KEYS
    s1 = _dot_nt(qp_ref[:, :half].astype(BF16), k1_ref[...].astype(BF16))
    s2 = _dot_nt(qp_ref[:, half:].astype(BF16), k2_ref[...].astype(BF16))
    v1 = _top_values(s1, PEER_TOPK)
    v2 = _top_values(s2, PEER_TOPK)
    lane = lax.broadcasted_iota(jnp.int32, (tm, PEER_TOPK * PEER_TOPK), 1)
    a1 = jnp.zeros((tm, PEER_TOPK * PEER_TOPK), F32)
    b2 = jnp.zeros((tm, PEER_TOPK * PEER_TOPK), F32)
    for a in range(PEER_TOPK):
        a1 = jnp.where((lane >> 4) == a, v1[a], a1)
        b2 = jnp.where((lane & (PEER_TOPK - 1)) == a, v2[a], b2)
    cand = a1 + b2
    top = _top_values(cand, PEER_TOPK)
    cmax, tau = top[0], top[-1]
    z = jnp.sum(jnp.where(cand >= tau, jnp.exp(cand - cmax), 0.0), axis=1, keepdims=True)
    s1t_ref[...] = s1.T
    s2t_ref[...] = s2.T
    e1t_ref[...] = (jnp.exp(s1 - v1[0]) / z).T
    e2t_ref[...] = jnp.exp(s2 - v2[0]).T
    tau_ref[...] = jnp.broadcast_to(tau, (tm, LANES)).T[:1, :]


def _peer_sel_call(qp, k1, k2, tm):
    t = qp.shape[0]
    kd = 2 * N_KEYS
    key_spec = pl.BlockSpec((N_KEYS, N_KEYS), lambda i, h: (0, 0))
    st = pl.BlockSpec((None, N_KEYS, tm), lambda i, h: (h, 0, i))
    st_shape = SDS((PEER_HEADS, N_KEYS, t), F32)
    return pl.pallas_call(
        functools.partial(_peer_sel_kernel, tm=tm),
        grid=(t // tm, PEER_HEADS),
        in_specs=[pl.BlockSpec((tm, kd), lambda i, h: (i, h)), key_spec, key_spec],
        out_specs=[st, st, st, st, pl.BlockSpec((None, 1, tm), lambda i, h: (h, 0, i))],
        out_shape=[st_shape, st_shape, st_shape, st_shape, SDS((PEER_HEADS, 1, t), F32)],
        compiler_params=_cp("parallel", "arbitrary"),
        name="peer_select",
    )(qp, k1, k2)


def _peer_dense_kernel(h2t_ref, u_ref, vt_ref, s1t_ref, s2t_ref, e1t_ref, e2t_ref, tau_ref, o_ref, *, te):
    j = pl.program_id(1)

    @pl.when(j == 0)
    def _():
        o_ref[...] = jnp.zeros_like(o_ref)

    at = jnp.dot(u_ref[...], h2t_ref[...], preferred_element_type=F32)
    act = 0.5 * at * (1.0 + lax.erf(at * (1.0 / math.sqrt(2.0))))
    parts = []
    for r in range(te // N_KEYS):
        i1 = j * (te // N_KEYS) + r
        w = jnp.zeros((N_KEYS, at.shape[1]), F32)
        for h in range(PEER_HEADS):
            s1r = s1t_ref[h, pl.ds(i1, 1), :]
            e1r = e1t_ref[h, pl.ds(i1, 1), :]
            mask = (s1r + s2t_ref[h]) >= tau_ref[h]
            w = w + jnp.where(mask, e1r * e2t_ref[h], 0.0)
        parts.append(w * act[r * N_KEYS:(r + 1) * N_KEYS, :])
    g = jnp.concatenate(parts, axis=0).astype(BF16)
    o_ref[...] += jnp.dot(vt_ref[...], g, preferred_element_type=F32)


def _peer_dense_call(h2t, u_bf, vt_bf, s1t, s2t, e1t, e2t, tau, tm, te):
    d, t = h2t.shape
    ne = u_bf.shape[0]
    st = pl.BlockSpec((PEER_HEADS, N_KEYS, tm), lambda i, j: (0, 0, i))
    return pl.pallas_call(
        functools.partial(_peer_dense_kernel, te=te),
        grid=(t // tm, ne // te),
        in_specs=[
            pl.BlockSpec((d, tm), lambda i, j: (0, i)),
            pl.BlockSpec((te, d), lambda i, j: (j, 0)),
            pl.BlockSpec((d, te), lambda i, j: (0, j)),
            st, st, st, st,
            pl.BlockSpec((PEER_HEADS, 1, tm), lambda i, j: (0, 0, i)),
        ],
        out_specs=pl.BlockSpec((d, tm), lambda i, j: (0, i)),
        out_shape=SDS((d, t), F32),
        compiler_params=_cp("parallel", "arbitrary"),
        name="peer_dense",
    )(h2t, u_bf, vt_bf, s1t, s2t, e1t, e2t, tau)


def _rope_tables(pos):
    half = HEAD_DIM // 2
    inv = ROPE_THETA ** (-jnp.arange(half, dtype=F32) / half)
    ang = pos.astype(F32)[:, None] * inv[None, :]
    cos, sin = jnp.cos(ang), jnp.sin(ang)
    return jnp.concatenate([cos, cos], axis=1), jnp.concatenate([-sin, sin], axis=1)


def _layer_weights(l, w_in, b_f, w_o_fox, w_o_diff, w_out, peer_w_q, peer_u, peer_v):
    wl = w_in[l]
    c_fl = 3 * FOX_W
    w_r = jnp.concatenate(
        [wl[:, :c_fl], wl[:, c_fl + FOX_HEADS:], jnp.pad(wl[:, c_fl:c_fl + FOX_HEADS], ((0, 0), (0, LANES - FOX_HEADS)))],
        axis=1).astype(BF16)
    bf_pad = jnp.pad(b_f[l][None, :], ((0, 0), (0, LANES - FOX_HEADS)))
    return dict(
        w_r=w_r, bf_pad=bf_pad,
        wf=w_o_fox[l].astype(BF16), wd=w_o_diff[l].astype(BF16), wo=w_out[l].astype(BF16),
        pq=peer_w_q[l].astype(BF16), u=peer_u[l].astype(BF16), vt=peer_v[l].T.astype(BF16),
    )


def _tail(x2d, mods, proj, fo, dn, lw, vecs, alpha, tm, peer_tm, peer_te):
    _, _, gate1, shift2, scale2, gate2 = mods
    t = x2d.shape[0]
    y = _mix_call(fo, dn, proj, lw["wf"], lw["wd"], tm)
    x1 = _resid_ln_call(x2d, y, gate1, lw["wo"], vecs["ln1_g"], vecs["ln1_b"], tm, alpha)
    tp = -(-t // peer_tm) * peer_tm
    pad = tp - t
    x1p = jnp.pad(x1, ((0, pad), (0, 0))) if pad else x1
    sc2, sh2 = scale2, shift2
    if pad and sc2.shape[0] != 1:
        sc2 = jnp.pad(sc2, ((0, pad), (0, 0)))
        sh2 = jnp.pad(sh2, ((0, pad), (0, 0)))
    qp, h2t = _lnmod_mm_call(x1p, sc2, sh2, lw["pq"], peer_tm, 1024, emit_ht=True)
    s1t, s2t, e1t, e2t, tau = _peer_sel_call(qp, vecs["pk1"], vecs["pk2"], min(peer_tm, 256))
    ffn_t = _peer_dense_call(h2t, lw["u"], lw["vt"], s1t, s2t, e1t, e2t, tau, peer_tm, peer_te)
    if pad:
        ffn_t = ffn_t[:, :t]
    return _final_ln_call(x1, ffn_t, gate2, vecs["ln2_g"], vecs["ln2_b"], tm, alpha)


def kernel(x_prompt, x_sample, cache_fox_k, cache_fox_v, cache_fox_logf, cache_diff_k, cache_diff_v, page_table, c_prompt, c_sample, w_ada, b_ada, w_in, b_f, lambda_q1, lambda_k1, lambda_q2, lambda_k2, diff_subln_w, w_o_fox, w_o_diff, w_out, ln1_g, ln1_b, peer_w_q, peer_sub_k1, peer_sub_k2, peer_u, peer_v, ln2_g, ln2_b):
    depth = w_ada.shape[0]
    bp, sp, d = x_prompt.shape
    bs, ss, _ = x_sample.shape
    assert bp == 1 and ss == 1, "one prompt sequence and one new token per sample sequence"
    n_pages = page_table.shape[1]
    page = cache_fox_k.shape[2]
    n_pool = cache_fox_k.shape[1]
    n_past = n_pages * page
    alpha = (2 * depth) ** 0.25
    dv = 2 * HEAD_DIM
    pg = 8

    cos_p, sin_p = _rope_tables(jnp.arange(sp))
    cos_s, sin_s = _rope_tables(n_past + jnp.arange(ss))
    lf_flat = cache_fox_logf.reshape(depth, n_pool, 1, page * FOX_HEADS)

    c_all = jnp.concatenate([c_prompt, c_sample], axis=0)
    n_c = c_all.shape[0]
    c_all = jnp.pad(c_all, ((0, -n_c % 8), (0, 0)))

    xp = x_prompt.reshape(sp, d)
    xs = x_sample.reshape(bs, d)
    st_p, st_s = [], []
    for l in range(depth):
        lam_init = 0.8 - 0.6 * math.exp(-0.3 * l)
        lw = _layer_weights(l, w_in, b_f, w_o_fox, w_o_diff, w_out, peer_w_q, peer_u, peer_v)
        vecs = dict(ln1_g=ln1_g[l][None], ln1_b=ln1_b[l][None], ln2_g=ln2_g[l][None], ln2_b=ln2_b[l][None],
                    pk1=peer_sub_k1[l], pk2=peer_sub_k2[l])
        lam_vecs = [v[l][None] for v in (lambda_q1, lambda_k1, lambda_q2, lambda_k2)]
        subln = diff_subln_w[l][None]

        mod = _ada_call(c_all, w_ada, b_ada, l)
        mods_p = [mod[0:bp, i * d:(i + 1) * d] for i in range(6)]
        mods_s = [mod[bp:bp + bs, i * d:(i + 1) * d] for i in range(6)]

        proj = _lnmod_mm_call(xp, mods_p[1], mods_p[0], lw["w_r"], 512, 1152)
        dq_r, dk_r, logf, cum, cumt = _post_call(proj, cos_p, sin_p, lw["bf_pad"], 256, True)
        fo = _fox_flash_call(proj, cum, cumt, 512)
        dn = _diff_flash_call(dq_r, dk_r, proj, lam_vecs, subln, 512, lam_init)
        st_p.append((proj[:, FOX_W:2 * FOX_W].reshape(bp, sp, FOX_HEADS, HEAD_DIM),
                     proj[:, 2 * FOX_W:3 * FOX_W].reshape(bp, sp, FOX_HEADS, HEAD_DIM),
                     logf.reshape(bp, sp, FOX_HEADS),
                     dk_r.reshape(bp, sp, DIFF_HEADS, dv),
                     proj[:, 5 * FOX_W:6 * FOX_W].reshape(bp, sp, DIFF_HEADS, dv)))
        xp = _tail(xp, mods_p, proj, fo, dn, lw, vecs, alpha, 256, 512, 512)

        proj_s = _lnmod_mm_call(xs, mods_s[1], mods_s[0], lw["w_r"], bs, 1152)
        dq_s, dk_s, logf_s = _post_call(proj_s, cos_s, sin_s, lw["bf_pad"], bs, False)
        fk_s = proj_s[:, FOX_W:2 * FOX_W].reshape(bs, FOX_HEADS, HEAD_DIM)
        fv_s = proj_s[:, 2 * FOX_W:3 * FOX_W].reshape(bs, FOX_HEADS, HEAD_DIM)
        dv_s = proj_s[:, 5 * FOX_W:6 * FOX_W].reshape(bs, DIFF_HEADS, dv)
        lfn_cls = jnp.tile(logf_s, (1, LANES // FOX_HEADS)).reshape(bs, 1, LANES)
        fo_s = _fox_dec_call(l, page_table, proj_s[:, :FOX_W].reshape(bs, FOX_HEADS, HEAD_DIM), fk_s, fv_s, lfn_cls,
                             cache_fox_k, cache_fox_v, lf_flat, pg)
        dn_s = _diff_dec_call(l, page_table, dq_s.reshape(bs, N_MAPS, HEAD_DIM), dk_s.reshape(bs, N_MAPS, HEAD_DIM),
                              jnp.repeat(dv_s, 2, axis=1), lam_vecs, subln, cache_diff_k, cache_diff_v, pg, lam_init)
        dn_s = dn_s[:, ::2, :].reshape(bs, FOX_W)
        st_s.append((fk_s.reshape(bs, ss, FOX_HEADS, HEAD_DIM), fv_s.reshape(bs, ss, FOX_HEADS, HEAD_DIM),
                     logf_s.reshape(bs, ss, FOX_HEADS), dk_s.reshape(bs, ss, DIFF_HEADS, dv),
                     dv_s.reshape(bs, ss, DIFF_HEADS, dv)))
        xs = _tail(xs, mods_s, proj_s, fo_s.reshape(bs, FOX_W), dn_s, lw, vecs, alpha, bs, LANES, 512)

    outs_p = [jnp.stack([s[i] for s in st_p]) for i in range(5)]
    outs_s = [jnp.stack([s[i] for s in st_s]) for i in range(5)]
    return (xp.reshape(bp, sp, d), xs.reshape(bs, ss, d), *outs_p, *outs_s)
```

```python
import functools
import math

import jax
import jax.numpy as jnp
import numpy as np
from jax import lax
from jax.experimental import pallas as pl
from jax.experimental.pallas import tpu as pltpu

F32 = jnp.float32
BF16 = jnp.bfloat16
NEG_INF = float("-inf")
POS_INF = float("inf")

LN_EPS = 1e-5
ROPE_THETA = 10000.0
HEAD_DIM = 128
FOX_HEADS = 8
DIFF_HEADS = 4
N_MAPS = 8
FOX_W = FOX_HEADS * HEAD_DIM
DIFF_V = 2 * HEAD_DIM
PEER_HEADS = 8
PEER_TOPK = 16
N_KEYS = 128
ATTN_SCALE = HEAD_DIM ** -0.5
LANES = 128
MXU_DIM = 256
VMEM_LIMIT = 48 * 1024 * 1024

SDS = jax.ShapeDtypeStruct


def _cp(*sem):
    return pltpu.CompilerParams(dimension_semantics=sem, vmem_limit_bytes=VMEM_LIMIT)


def _ln(x):
    mu = jnp.mean(x, axis=-1, keepdims=True)
    xc = x - mu
    var = jnp.mean(xc * xc, axis=-1, keepdims=True)
    return xc * lax.rsqrt(var + LN_EPS)


def _dot_nt(a, b):
    return lax.dot_general(a, b, (((1,), (1,)), ((), ())), preferred_element_type=F32)


def _dot(a, b):
    return jnp.dot(a, b, preferred_element_type=F32)


def _rep_chunks(x, k):
    return x if k == 1 else jnp.concatenate([x] * k, axis=1)


def _ada_kernel(c_ref, w_ref, b_ref, o_ref):
    c = c_ref[...]
    a = (c * jax.nn.sigmoid(c)).astype(BF16)
    o_ref[...] = _dot(a, w_ref[...].astype(BF16)) + b_ref[...]


def _ada_call(c_all, w_ada, b_ada, layer):
    mp, d = c_all.shape
    n = w_ada.shape[2]
    tn = 1024
    return pl.pallas_call(
        _ada_kernel,
        grid=(n // tn,),
        in_specs=[
            pl.BlockSpec((mp, d), lambda j: (0, 0)),
            pl.BlockSpec((None, d, tn), lambda j: (layer, 0, j)),
            pl.BlockSpec((None, 1, tn), lambda j: (layer, 0, j)),
        ],
        out_specs=pl.BlockSpec((mp, tn), lambda j: (0, j)),
        out_shape=SDS((mp, n), F32),
        compiler_params=_cp("arbitrary"),
        name="ada",
    )(c_all, w_ada, b_ada.reshape(b_ada.shape[0], 1, n))


def _lnmod_mm_kernel(x_ref, sc_ref, sh_ref, w_ref, o_ref, h_scr):
    @pl.when(pl.program_id(1) == 0)
    def _():
        h = _ln(x_ref[...]) * (1.0 + sc_ref[...]) + sh_ref[...]
        h_scr[...] = h.astype(BF16)

    o_ref[...] = _dot(h_scr[...], w_ref[...])


def _mod_spec2(r, tm, d):
    return pl.BlockSpec((1, d), lambda i, j: (0, 0)) if r == 1 else pl.BlockSpec((tm, d), lambda i, j: (i, 0))


def _lnmod_mm_call(x, sc, sh, w, tm, tn):
    t, d = x.shape
    n = w.shape[1]
    mod_spec = _mod_spec2(sc.shape[0], tm, d)
    return pl.pallas_call(
        _lnmod_mm_kernel,
        grid=(t // tm, n // tn),
        in_specs=[
            pl.BlockSpec((tm, d), lambda i, j: (i, 0)),
            mod_spec,
            mod_spec,
            pl.BlockSpec((d, tn), lambda i, j: (0, j)),
        ],
        out_specs=pl.BlockSpec((tm, tn), lambda i, j: (i, j)),
        out_shape=SDS((t, n), F32),
        scratch_shapes=[pltpu.VMEM((tm, d), BF16)],
        compiler_params=_cp("parallel", "arbitrary"),
        name="lnmod_mm",
    )(x, sc, sh, w)


def _rope(x, c, s):
    return x * c + pltpu.roll(x, HEAD_DIM // 2, axis=1) * s


def _log_sigmoid(z):
    return jnp.minimum(z, 0.0) - jnp.log1p(jnp.exp(-jnp.abs(z)))


def _post_kernel(dq_ref, dk_ref, fl_ref, cos_ref, sin_ref, bf_ref, dqo_ref, dko_ref, lf_ref):
    c = cos_ref[...]
    s = sin_ref[...]
    for j in range(N_MAPS):
        sl = slice(j * HEAD_DIM, (j + 1) * HEAD_DIM)
        dqo_ref[:, sl] = _rope(dq_ref[:, sl], c, s)
        dko_ref[:, sl] = _rope(dk_ref[:, sl], c, s)
    lf_ref[...] = _log_sigmoid(fl_ref[...] + bf_ref[...])[:, :FOX_HEADS]


def _post_call(proj, cos2, sin2, bf_pad, tm):
    t = proj.shape[0]
    tab_spec = pl.BlockSpec((1, LANES), lambda i: (0, 0))
    row = pl.BlockSpec((tm, FOX_W), lambda i: (i, 0))
    return pl.pallas_call(
        _post_kernel,
        grid=(t // tm,),
        in_specs=[
            pl.BlockSpec((tm, FOX_W), lambda i: (i, 3)),
            pl.BlockSpec((tm, FOX_W), lambda i: (i, 4)),
            pl.BlockSpec((tm, LANES), lambda i: (i, 80)),
            tab_spec, tab_spec, tab_spec,
        ],
        out_specs=[row, row, pl.BlockSpec((tm, FOX_HEADS), lambda i: (i, 0))],
        out_shape=[SDS((t, FOX_W), F32), SDS((t, FOX_W), F32), SDS((t, FOX_HEADS), F32)],
        compiler_params=_cp("arbitrary"),
        name="post_proj",
    )(proj, proj, proj, cos2, sin2, bf_pad)


def _post_prompt_kernel(fq_ref, fk_ref, fv_ref, dq_ref, dk_ref, dv_ref, fl_ref, cos_ref, sin_ref, bf_ref,
                        fko_ref, fvo_ref, dko_ref, dvo_ref, lf_ref,
                        fqt_ref, fkb_ref, fvt_ref, dqt_ref, dkb_ref, dvt_ref, ckb_ref, cumt_ref,
                        carry_scr, *, tm):
    c = cos_ref[...]
    s = sin_ref[...]
    for j in range(N_MAPS):
        sl = slice(j * HEAD_DIM, (j + 1) * HEAD_DIM)
        half = slice((j % 2) * HEAD_DIM, (j % 2 + 1) * HEAD_DIM)
        dk = _rope(dk_ref[:, sl], c, s)
        dko_ref[:, j // 2, half] = dk
        dkb_ref[:, sl] = dk.astype(BF16)
        dqt_ref[sl, :] = (_rope(dq_ref[:, sl], c, s) * ATTN_SCALE).T.astype(BF16)
        fqt_ref[sl, :] = (fq_ref[:, sl] * ATTN_SCALE).T.astype(BF16)
        fk = fk_ref[:, sl]
        fko_ref[:, j, :] = fk
        fkb_ref[:, sl] = fk.astype(BF16)
        fv = fv_ref[:, sl]
        fvo_ref[:, j, :] = fv
        fvt_ref[sl, :] = fv.T.astype(BF16)
        dv = dv_ref[:, sl]
        dvo_ref[:, j // 2, half] = dv
        dvt_ref[sl, :] = dv.T.astype(BF16)
    lf = _log_sigmoid(fl_ref[...] + bf_ref[...])
    lf_ref[...] = lf[:, :FOX_HEADS]

    @pl.when(pl.program_id(0) == 0)
    def _():
        carry_scr[...] = jnp.zeros_like(carry_scr)

    row = lax.broadcasted_iota(jnp.int32, (tm, tm), 0)
    col = lax.broadcasted_iota(jnp.int32, (tm, tm), 1)
    tri = jnp.where(col <= row, 1.0, 0.0).astype(BF16)
    hi = lf.astype(BF16)
    r1 = lf - hi.astype(F32)
    mid = r1.astype(BF16)
    lo = (r1 - mid.astype(F32)).astype(BF16)
    cum = (_dot(tri, hi) + _dot(tri, mid) + _dot(tri, lo)) + carry_scr[...]
    carry_scr[...] = cum[tm - 1:tm, :]
    cumt_ref[...] = cum.T[:FOX_HEADS, :]
    for h in range(FOX_HEADS):
        ckb_ref[h] = jnp.broadcast_to(cum[:, h:h + 1], (tm, LANES))


def _post_prompt_call(proj, cos2, sin2, bf_pad, tm):
    t = proj.shape[0]
    col = lambda c: pl.BlockSpec((tm, FOX_W), lambda i: (i, c))
    tab = pl.BlockSpec((tm, LANES), lambda i: (i, 0))
    row_f32 = pl.BlockSpec((tm, FOX_W), lambda i: (i, 0))
    tr = pl.BlockSpec((FOX_W, tm), lambda i: (0, i))
    fox_st = pl.BlockSpec((tm, FOX_HEADS, HEAD_DIM), lambda i: (i, 0, 0))
    diff_st = pl.BlockSpec((tm, DIFF_HEADS, DIFF_V), lambda i: (i, 0, 0))
    return pl.pallas_call(
        functools.partial(_post_prompt_kernel, tm=tm),
        grid=(t // tm,),
        in_specs=[col(0), col(1), col(2), col(3), col(4), col(5),
                  pl.BlockSpec((tm, LANES), lambda i: (i, 80)), tab, tab,
                  pl.BlockSpec((1, LANES), lambda i: (0, 0))],
        out_specs=[fox_st, fox_st, diff_st, diff_st, pl.BlockSpec((tm, FOX_HEADS), lambda i: (i, 0)),
                   tr, row_f32, tr, tr, row_f32, tr,
                   pl.BlockSpec((FOX_HEADS, tm, LANES), lambda i: (0, i, 0)),
                   pl.BlockSpec((FOX_HEADS, tm), lambda i: (0, i))],
        out_shape=[SDS((t, FOX_HEADS, HEAD_DIM), F32), SDS((t, FOX_HEADS, HEAD_DIM), F32),
                   SDS((t, DIFF_HEADS, DIFF_V), F32), SDS((t, DIFF_HEADS, DIFF_V), F32), SDS((t, FOX_HEADS), F32),
                   SDS((FOX_W, t), BF16), SDS((t, FOX_W), BF16), SDS((FOX_W, t), BF16),
                   SDS((FOX_W, t), BF16), SDS((t, FOX_W), BF16), SDS((FOX_W, t), BF16),
                   SDS((FOX_HEADS, t, LANES), F32), SDS((FOX_HEADS, t), F32)],
        scratch_shapes=[pltpu.VMEM((1, LANES), F32)],
        compiler_params=_cp("arbitrary"),
        name="post_prompt",
    )(proj, proj, proj, proj, proj, proj, proj, cos2, sin2, bf_pad)


def _tri_pairs(n):
    qi = np.concatenate([np.full(i + 1, i, np.int32) for i in range(n)])
    ki = np.concatenate([np.arange(i + 1, dtype=np.int32) for i in range(n)])
    return jnp.asarray(qi), jnp.asarray(ki)


def _flash_update(st, vt_bf, m_scr, l_scr, acc_scr, idx, rows):
    m_prev = m_scr[idx:idx + 1, :]
    m_new = jnp.maximum(m_prev, jnp.max(st, axis=0, keepdims=True))
    alpha = jnp.exp(m_prev - m_new)
    p = jnp.exp(st - m_new)
    l_scr[idx:idx + 1, :] = alpha * l_scr[idx:idx + 1, :] + jnp.sum(p, axis=0, keepdims=True)
    acc_scr[rows, :] = alpha * acc_scr[rows, :] + _dot(vt_bf, p.astype(BF16))
    m_scr[idx:idx + 1, :] = m_new


def _flash_init(m_scr, l_scr, acc_scr):
    m_scr[...] = jnp.full_like(m_scr, NEG_INF)
    l_scr[...] = jnp.zeros_like(l_scr)
    acc_scr[...] = jnp.zeros_like(acc_scr)


def _causal_keep_t(tk, tq):
    kpos = lax.broadcasted_iota(jnp.int32, (tk, tq), 0)
    qpos = lax.broadcasted_iota(jnp.int32, (tk, tq), 1)
    return kpos <= qpos


def _fox_flash_kernel(qi_tab, ki_tab, qt_ref, k_ref, vt_ref, cq_ref, ckb_ref, o_ref, m_scr, l_scr, acc_scr, *, tq, tk):
    pid = pl.program_id(0)
    qi = qi_tab[pid]
    ki = ki_tab[pid]

    @pl.when(ki == 0)
    def _():
        _flash_init(m_scr, l_scr, acc_scr)

    def body(masked):
        keep = _causal_keep_t(tk, tq) if masked else None
        for h in range(FOX_HEADS):
            rows = slice(h * HEAD_DIM, (h + 1) * HEAD_DIM)
            st = _dot(k_ref[:, rows], qt_ref[rows, :])
            st = st + (cq_ref[h:h + 1, :] - _rep_chunks(ckb_ref[h], tq // LANES))
            if masked:
                st = jnp.where(keep, st, NEG_INF)
            _flash_update(st, vt_ref[rows, :], m_scr, l_scr, acc_scr, h, rows)

    @pl.when(ki < qi)
    def _():
        body(False)

    @pl.when(ki == qi)
    def _():
        body(True)
        for h in range(FOX_HEADS):
            rows = slice(h * HEAD_DIM, (h + 1) * HEAD_DIM)
            o_ref[:, rows] = (acc_scr[rows, :] / l_scr[h:h + 1, :]).T.astype(o_ref.dtype)


def _fox_flash_call(fqt, fkb, fvt, cumt, ckb, tq):
    t = fkb.shape[0]
    nq = t // tq
    qi_tab, ki_tab = _tri_pairs(nq)
    grid_spec = pltpu.PrefetchScalarGridSpec(
        num_scalar_prefetch=2,
        grid=(int(qi_tab.shape[0]),),
        in_specs=[
            pl.BlockSpec((FOX_W, tq), lambda p, qt, kt: (0, qt[p])),
            pl.BlockSpec((tq, FOX_W), lambda p, qt, kt: (kt[p], 0)),
            pl.BlockSpec((FOX_W, tq), lambda p, qt, kt: (0, kt[p])),
            pl.BlockSpec((FOX_HEADS, tq), lambda p, qt, kt: (0, qt[p])),
            pl.BlockSpec((FOX_HEADS, tq, LANES), lambda p, qt, kt: (0, kt[p], 0)),
        ],
        out_specs=pl.BlockSpec((tq, FOX_W), lambda p, qt, kt: (qt[p], 0)),
        scratch_shapes=[
            pltpu.VMEM((N_MAPS, tq), F32),
            pltpu.VMEM((N_MAPS, tq), F32),
            pltpu.VMEM((FOX_W, tq), F32),
        ],
    )
    return pl.pallas_call(
        functools.partial(_fox_flash_kernel, tq=tq, tk=tq),
        grid_spec=grid_spec,
        out_shape=SDS((t, FOX_W), BF16),
        compiler_params=_cp("arbitrary"),
        name="fox_flash",
    )(qi_tab, ki_tab, fqt, fkb, fvt, cumt, ckb)


def _lambda_full(lq1_ref, lk1_ref, lq2_ref, lk2_ref, lam_init):
    a = jnp.exp(jnp.sum(lq1_ref[...] * lk1_ref[...], axis=1, keepdims=True))
    b = jnp.exp(jnp.sum(lq2_ref[...] * lk2_ref[...], axis=1, keepdims=True))
    return a - b + lam_init


def _diff_flash_kernel(qi_tab, ki_tab, qt_ref, k_ref, vt_ref, lq1_ref, lk1_ref, lq2_ref, lk2_ref, sub_ref,
                       o_ref, m_scr, l_scr, acc_scr, *, tq, tk, lam_init):
    pid = pl.program_id(0)
    qi = qi_tab[pid]
    ki = ki_tab[pid]

    @pl.when(ki == 0)
    def _():
        _flash_init(m_scr, l_scr, acc_scr)

    def body(masked):
        keep = _causal_keep_t(tk, tq) if masked else None
        for mp in range(N_MAPS):
            hd = mp // 2
            rows = slice(mp * HEAD_DIM, (mp + 1) * HEAD_DIM)
            st = _dot(k_ref[:, rows], qt_ref[rows, :])
            if masked:
                st = jnp.where(keep, st, NEG_INF)
            _flash_update(st, vt_ref[hd * DIFF_V:(hd + 1) * DIFF_V, :], m_scr, l_scr, acc_scr, mp,
                          slice(mp * DIFF_V, (mp + 1) * DIFF_V))

    @pl.when(ki < qi)
    def _():
        body(False)

    @pl.when(ki == qi)
    def _():
        body(True)
        lam = _lambda_full(lq1_ref, lk1_ref, lq2_ref, lk2_ref, lam_init)
        sub = _rep_chunks(sub_ref[...], tq // LANES)
        for hd in range(DIFF_HEADS):
            o1 = acc_scr[(2 * hd) * DIFF_V:(2 * hd + 1) * DIFF_V, :] / l_scr[2 * hd:2 * hd + 1, :]
            o2 = acc_scr[(2 * hd + 1) * DIFF_V:(2 * hd + 2) * DIFF_V, :] / l_scr[2 * hd + 1:2 * hd + 2, :]
            d = o1 - lam * o2
            y = d * lax.rsqrt(jnp.mean(d * d, axis=0, keepdims=True) + LN_EPS)
            o_ref[:, hd * DIFF_V:(hd + 1) * DIFF_V] = ((y * sub) * (1.0 - lam_init)).T.astype(o_ref.dtype)


def _diff_flash_call(dqt, dkb, dvt, lam_vecs, subln_rep, tq, lam_init):
    t = dkb.shape[0]
    nq = t // tq
    qi_tab, ki_tab = _tri_pairs(nq)
    vec = pl.BlockSpec((1, HEAD_DIM), lambda p, qt, kt: (0, 0))
    grid_spec = pltpu.PrefetchScalarGridSpec(
        num_scalar_prefetch=2,
        grid=(int(qi_tab.shape[0]),),
        in_specs=[
            pl.BlockSpec((FOX_W, tq), lambda p, qt, kt: (0, qt[p])),
            pl.BlockSpec((tq, FOX_W), lambda p, qt, kt: (kt[p], 0)),
            pl.BlockSpec((FOX_W, tq), lambda p, qt, kt: (0, kt[p])),
            vec, vec, vec, vec,
            pl.BlockSpec((DIFF_V, LANES), lambda p, qt, kt: (0, 0)),
        ],
        out_specs=pl.BlockSpec((tq, FOX_W), lambda p, qt, kt: (qt[p], 0)),
        scratch_shapes=[
            pltpu.VMEM((N_MAPS, tq), F32),
            pltpu.VMEM((N_MAPS, tq), F32),
            pltpu.VMEM((N_MAPS * DIFF_V, tq), F32),
        ],
    )
    return pl.pallas_call(
        functools.partial(_diff_flash_kernel, tq=tq, tk=tq, lam_init=lam_init),
        grid_spec=grid_spec,
        out_shape=SDS((t, FOX_W), BF16),
        compiler_params=_cp("arbitrary"),
        name="diff_flash",
    )(qi_tab, ki_tab, dqt, dkb, dvt, *lam_vecs, subln_rep)


def _class_allreduce(x, op, period):
    s = period
    while s < LANES:
        x = op(x, pltpu.roll(x, s, axis=1))
        s *= 2
    return x


def _fold_chunks(x, op):
    out = x[:, :LANES]
    for c in range(1, x.shape[1] // LANES):
        out = op(out, x[:, c * LANES:(c + 1) * LANES])
    return out


def _suffix_incl(x, period):
    n = x.shape[1]
    lane = lax.broadcasted_iota(jnp.int32, x.shape, 1)
    s = period
    while s < n:
        y = pltpu.roll(x, n - s, axis=1)
        x = x + jnp.where(lane < n - s, y, 0.0)
        s *= 2
    return x


def _fox_dec_kernel(pt_ref, q_ref, kn_ref, vn_ref, lfn_ref, *refs, pg, nj):
    del pt_ref
    k_refs, v_refs, lf_refs = refs[:pg], refs[pg:2 * pg], refs[2 * pg:3 * pg]
    o_ref = refs[3 * pg]
    m_scr, l_scr, acc_scr, carry_scr = refs[3 * pg + 1:]
    j = pl.program_id(1)
    nh = FOX_HEADS
    w = PAGE_ROWS_FOX
    sub = lax.broadcasted_iota(jnp.int32, (nh, w), 0)
    lane = lax.broadcasted_iota(jnp.int32, (nh, w), 1)
    diag = sub == (lane & (nh - 1))
    diag1 = diag[:, :LANES]

    def to_sub(x_cls):
        return jnp.max(jnp.where(diag1, jnp.broadcast_to(x_cls, (nh, LANES)), NEG_INF), axis=1, keepdims=True)

    @pl.when(j == 0)
    def _():
        m_scr[...] = jnp.full_like(m_scr, NEG_INF)
        l_scr[...] = jnp.zeros_like(l_scr)
        acc_scr[...] = jnp.zeros_like(acc_scr)
        carry_scr[...] = lfn_ref[...]

    qb = (q_ref[...] * ATTN_SCALE).astype(BF16)

    lf = jnp.concatenate([r[...] for r in lf_refs], axis=0)
    tot = _class_allreduce(_fold_chunks(lf, jnp.add), jnp.add, nh)
    run = carry_scr[...]
    before = []
    for i in range(pg):
        before.append(run)
        run = run + tot[i:i + 1, :]
    carry_scr[...] = run
    bias = (_suffix_incl(lf, nh) - lf) + _rep_chunks(jnp.concatenate(before, axis=0), w // LANES)

    rows = []
    for i in range(pg):
        k2 = k_refs[i][...].reshape(w, HEAD_DIM).astype(BF16)
        st = _dot_nt(qb, k2)
        rows.append(jnp.sum(jnp.where(diag, st, 0.0), axis=0, keepdims=True))
    s = jnp.concatenate(rows, axis=0) + bias

    mc = _class_allreduce(_fold_chunks(jnp.max(s, axis=0, keepdims=True), jnp.maximum), jnp.maximum, nh)
    m_old = m_scr[...]
    m_new = jnp.maximum(m_old, mc)
    alpha = jnp.exp(m_old - m_new)
    p = jnp.exp(s - _rep_chunks(m_new, w // LANES))
    l_scr[...] = l_scr[...] * _rep_chunks(alpha, w // LANES) + p

    acc_c = jnp.zeros((nh, HEAD_DIM), F32)
    for i in range(pg):
        pm = jnp.where(diag, jnp.broadcast_to(p[i:i + 1, :], (nh, w)), 0.0).astype(BF16)
        v2 = v_refs[i][...].reshape(w, HEAD_DIM).astype(BF16)
        acc_c = acc_c + _dot(pm, v2)
    acc_scr[...] = acc_scr[...] * to_sub(alpha) + acc_c
    m_scr[...] = m_new

    @pl.when(j == nj - 1)
    def _():
        kn = kn_ref[...].astype(BF16).astype(F32)
        s_self = jnp.sum(qb.astype(F32) * kn, axis=1, keepdims=True)
        s_cls = jnp.sum(jnp.where(diag1, jnp.broadcast_to(s_self, (nh, LANES)), 0.0), axis=0, keepdims=True)
        m_fin = jnp.maximum(m_new, s_cls)
        a2 = jnp.exp(m_new - m_fin)
        p_self = jnp.exp(s_cls - m_fin)
        l_cls = _class_allreduce(_fold_chunks(jnp.sum(l_scr[...], axis=0, keepdims=True), jnp.add), jnp.add, nh)
        l_cls = l_cls * a2 + p_self
        o = (acc_scr[...] * to_sub(a2) + to_sub(p_self) * vn_ref[...]) / to_sub(l_cls)
        o_ref[...] = o


PAGE_ROWS_FOX = 128 * FOX_HEADS
PAGE_ROWS_DIFF = 128 * DIFF_HEADS


def _fox_dec_call(layer, page_table, q, k_new, v_new, lfn_cls, cache_k, cache_v, cache_lf_flat, pg):
    b = q.shape[0]
    n_pages = page_table.shape[1]
    page = cache_k.shape[2]
    nj = n_pages // pg

    def page_idx(i):
        return lambda bb, j, pt: (layer, pt[bb, n_pages - 1 - (j * pg + i)], 0, 0, 0)

    def lf_idx(i):
        return lambda bb, j, pt: (layer, pt[bb, n_pages - 1 - (j * pg + i)], 0, 0)

    tok = pl.BlockSpec((None, FOX_HEADS, HEAD_DIM), lambda bb, j, pt: (bb, 0, 0))
    in_specs = [tok, tok, tok, pl.BlockSpec((None, 1, LANES), lambda bb, j, pt: (bb, 0, 0))]
    in_specs += [pl.BlockSpec((None, None, page, FOX_HEADS, HEAD_DIM), page_idx(i)) for i in range(pg)]
    in_specs += [pl.BlockSpec((None, None, page, FOX_HEADS, HEAD_DIM), page_idx(i)) for i in range(pg)]
    in_specs += [pl.BlockSpec((None, None, 1, page * FOX_HEADS), lf_idx(i)) for i in range(pg)]
    grid_spec = pltpu.PrefetchScalarGridSpec(
        num_scalar_prefetch=1,
        grid=(b, nj),
        in_specs=in_specs,
        out_specs=pl.BlockSpec((None, FOX_HEADS, HEAD_DIM), lambda bb, j, pt: (bb, 0, 0)),
        scratch_shapes=[
            pltpu.VMEM((1, LANES), F32),
            pltpu.VMEM((pg, page * FOX_HEADS), F32),
            pltpu.VMEM((FOX_HEADS, HEAD_DIM), F32),
            pltpu.VMEM((1, LANES), F32),
        ],
    )
    return pl.pallas_call(
        functools.partial(_fox_dec_kernel, pg=pg, nj=nj),
        grid_spec=grid_spec,
        out_shape=SDS((b, FOX_HEADS, HEAD_DIM), F32),
        compiler_params=_cp("parallel", "arbitrary"),
        name="fox_decode",
    )(page_table, q, k_new, v_new, lfn_cls, *([cache_k] * pg), *([cache_v] * pg), *([cache_lf_flat] * pg))


def _diff_dec_kernel(pt_ref, q_ref, kn_ref, vn_ref, lq1_ref, lk1_ref, lq2_ref, lk2_ref, sub_ref, *refs,
                     pg, nj, lam_init):
    del pt_ref
    k_refs, v_refs = refs[:pg], refs[pg:2 * pg]
    o_ref = refs[2 * pg]
    m_scr, l_scr, acc_scr = refs[2 * pg + 1:]
    j = pl.program_id(1)
    nh = DIFF_HEADS
    w = PAGE_ROWS_DIFF
    dv = DIFF_V
    sub = lax.broadcasted_iota(jnp.int32, (N_MAPS, w), 0)
    lane = lax.broadcasted_iota(jnp.int32, (N_MAPS, w), 1)
    same_head = (sub >> 1) == (lane & (nh - 1))
    sel = [jnp.logical_and(same_head, (sub & 1) == n) for n in range(2)]
    sel1 = [x[:, :LANES] for x in sel]

    def to_sub(x0, x1):
        y = jnp.where(sel1[0], jnp.broadcast_to(x0, (N_MAPS, LANES)),
                      jnp.where(sel1[1], jnp.broadcast_to(x1, (N_MAPS, LANES)), NEG_INF))
        return jnp.max(y, axis=1, keepdims=True)

    @pl.when(j == 0)
    def _():
        m_scr[...] = jnp.full_like(m_scr, NEG_INF)
        l_scr[...] = jnp.zeros_like(l_scr)
        acc_scr[...] = jnp.zeros_like(acc_scr)

    q8 = q_ref[...] * ATTN_SCALE
    odd = (lax.broadcasted_iota(jnp.int32, (N_MAPS, HEAD_DIM), 0) & 1) == 1
    qd = jnp.concatenate([jnp.where(odd, 0.0, q8), jnp.where(odd, q8, 0.0)], axis=1).astype(BF16)

    rows = [[], []]
    for i in range(pg):
        k2 = k_refs[i][...].reshape(w, dv).astype(BF16)
        st = _dot_nt(qd, k2)
        for n in range(2):
            rows[n].append(jnp.sum(jnp.where(sel[n], st, 0.0), axis=0, keepdims=True))
    s = [jnp.concatenate(rows[n], axis=0) for n in range(2)]

    p, alpha = [], []
    for n in range(2):
        mc = _class_allreduce(_fold_chunks(jnp.max(s[n], axis=0, keepdims=True), jnp.maximum), jnp.maximum, nh)
        m_old = m_scr[n]
        m_new = jnp.maximum(m_old, mc)
        a = jnp.exp(m_old - m_new)
        pn = jnp.exp(s[n] - _rep_chunks(m_new, w // LANES))
        l_scr[n] = l_scr[n] * _rep_chunks(a, w // LANES) + pn
        m_scr[n] = m_new
        p.append(pn)
        alpha.append(a)

    acc_c = jnp.zeros((N_MAPS, dv), F32)
    for i in range(pg):
        pm = jnp.where(sel[0], jnp.broadcast_to(p[0][i:i + 1, :], (N_MAPS, w)),
                       jnp.where(sel[1], jnp.broadcast_to(p[1][i:i + 1, :], (N_MAPS, w)), 0.0)).astype(BF16)
        v2 = v_refs[i][...].reshape(w, dv).astype(BF16)
        acc_c = acc_c + _dot(pm, v2)
    acc_scr[...] = acc_scr[...] * to_sub(alpha[0], alpha[1]) + acc_c

    @pl.when(j == nj - 1)
    def _():
        kn = kn_ref[...].astype(BF16).astype(F32)
        s_self = jnp.sum(q8.astype(BF16).astype(F32) * kn, axis=1, keepdims=True)
        a2, p_self, l_cls = [], [], []
        for n in range(2):
            s_cls = jnp.sum(jnp.where(sel1[n], jnp.broadcast_to(s_self, (N_MAPS, LANES)), 0.0),
                            axis=0, keepdims=True)
            m_new = m_scr[n]
            m_fin = jnp.maximum(m_new, s_cls)
            a = jnp.exp(m_new - m_fin)
            ps = jnp.exp(s_cls - m_fin)
            lc = _class_allreduce(_fold_chunks(jnp.sum(l_scr[n], axis=0, keepdims=True), jnp.add), jnp.add, nh)
            a2.append(a)
            p_self.append(ps)
            l_cls.append(lc * a + ps)
        o = (acc_scr[...] * to_sub(a2[0], a2[1]) + to_sub(p_self[0], p_self[1]) * vn_ref[...]) / to_sub(l_cls[0], l_cls[1])
        lam = _lambda_full(lq1_ref, lk1_ref, lq2_ref, lk2_ref, lam_init)
        d = o - lam * pltpu.roll(o, N_MAPS - 1, axis=0)
        y = d * lax.rsqrt(jnp.mean(d * d, axis=1, keepdims=True) + LN_EPS)
        o_ref[...] = (y * sub_ref[...]) * (1.0 - lam_init)


def _diff_dec_call(layer, page_table, q8, k_new8, v_new8, lam_vecs, subln, cache_k, cache_v, pg, lam_init):
    b = q8.shape[0]
    n_pages = page_table.shape[1]
    page = cache_k.shape[2]
    nj = n_pages // pg
    dv = DIFF_V

    def page_idx(i):
        return lambda bb, j, pt: (layer, pt[bb, n_pages - 1 - (j * pg + i)], 0, 0, 0)

    vec = pl.BlockSpec((1, HEAD_DIM), lambda bb, j, pt: (0, 0))
    tok = pl.BlockSpec((None, N_MAPS, HEAD_DIM), lambda bb, j, pt: (bb, 0, 0))
    in_specs = [tok, tok, pl.BlockSpec((None, N_MAPS, dv), lambda bb, j, pt: (bb, 0, 0)),
                vec, vec, vec, vec, pl.BlockSpec((1, dv), lambda bb, j, pt: (0, 0))]
    in_specs += [pl.BlockSpec((None, None, page, DIFF_HEADS, dv), page_idx(i)) for i in range(pg)]
    in_specs += [pl.BlockSpec((None, None, page, DIFF_HEADS, dv), page_idx(i)) for i in range(pg)]
    grid_spec = pltpu.PrefetchScalarGridSpec(
        num_scalar_prefetch=1,
        grid=(b, nj),
        in_specs=in_specs,
        out_specs=pl.BlockSpec((None, N_MAPS, dv), lambda bb, j, pt: (bb, 0, 0)),
        scratch_shapes=[
            pltpu.VMEM((2, 1, LANES), F32),
            pltpu.VMEM((2, pg, page * DIFF_HEADS), F32),
            pltpu.VMEM((N_MAPS, dv), F32),
        ],
    )
    return pl.pallas_call(
        functools.partial(_diff_dec_kernel, pg=pg, nj=nj, lam_init=lam_init),
        grid_spec=grid_spec,
        out_shape=SDS((b, N_MAPS, dv), F32),
        compiler_params=_cp("parallel", "arbitrary"),
        name="diff_decode",
    )(page_table, q8, k_new8, v_new8, *lam_vecs, subln, *([cache_k] * pg), *([cache_v] * pg))


def _mix_kernel(fo_ref, dn_ref, ga_ref, gb_ref, wf_ref, wd_ref, o_ref):
    yf = _dot(fo_ref[...].astype(BF16), wf_ref[...])
    yd = _dot(dn_ref[...].astype(BF16), wd_ref[...])
    o_ref[...] = (jax.nn.sigmoid(ga_ref[...]) * yf + jax.nn.sigmoid(gb_ref[...]) * yd).astype(o_ref.dtype)


def _mix_call(fo, dn, proj, wf, wd, tm):
    t = fo.shape[0]
    d = wf.shape[1]
    tn = 1024
    nb = d // tn
    return pl.pallas_call(
        _mix_kernel,
        grid=(t // tm, nb),
        in_specs=[
            pl.BlockSpec((tm, FOX_W), lambda i, j: (i, 0)),
            pl.BlockSpec((tm, FOX_W), lambda i, j: (i, 0)),
            pl.BlockSpec((tm, tn), lambda i, j: (i, 3 * nb + j)),
            pl.BlockSpec((tm, tn), lambda i, j: (i, 4 * nb + j)),
            pl.BlockSpec((FOX_W, tn), lambda i, j: (0, j)),
            pl.BlockSpec((FOX_W, tn), lambda i, j: (0, j)),
        ],
        out_specs=pl.BlockSpec((tm, tn), lambda i, j: (i, j)),
        out_shape=SDS((t, d), BF16),
        compiler_params=_cp("parallel", "arbitrary"),
        name="mix",
    )(fo, dn, proj, proj, wf, wd)


def _resid_ln_kernel(x_ref, y_ref, gate_ref, w_ref, g_ref, b_ref, o_ref, *, alpha):
    mix = _dot(y_ref[...], w_ref[...])
    z = alpha * x_ref[...] + gate_ref[...] * mix
    o_ref[...] = _ln(z) * g_ref[...] + b_ref[...]


def _mod_spec(r, tm, d):
    return pl.BlockSpec((1, d), lambda i: (0, 0)) if r == 1 else pl.BlockSpec((tm, d), lambda i: (i, 0))


def _resid_ln_call(x, y, gate, w, g, b, tm, alpha):
    t, d = x.shape
    row = pl.BlockSpec((tm, d), lambda i: (i, 0))
    vec = pl.BlockSpec((1, d), lambda i: (0, 0))
    return pl.pallas_call(
        functools.partial(_resid_ln_kernel, alpha=alpha),
        grid=(t // tm,),
        in_specs=[row, row, _mod_spec(gate.shape[0], tm, d), pl.BlockSpec((d, d), lambda i: (0, 0)), vec, vec],
        out_specs=row,
        out_shape=SDS((t, d), F32),
        compiler_params=_cp("parallel"),
        name="resid_ln1",
    )(x, y, gate, w, g, b)


def _final_ln_kernel(x_ref, ft_ref, gate_ref, g_ref, b_ref, o_ref, *, alpha):
    z = alpha * x_ref[...] + gate_ref[...] * ft_ref[...].T
    o_ref[...] = _ln(z) * g_ref[...] + b_ref[...]


def _final_ln_call(x, ffn_t, gate, g, b, tm, alpha):
    t, d = x.shape
    row = pl.BlockSpec((tm, d), lambda i: (i, 0))
    vec = pl.BlockSpec((1, d), lambda i: (0, 0))
    return pl.pallas_call(
        functools.partial(_final_ln_kernel, alpha=alpha),
        grid=(t // tm,),
        in_specs=[row, pl.BlockSpec((d, tm), lambda i: (0, i)), _mod_spec(gate.shape[0], tm, d), vec, vec],
        out_specs=row,
        out_shape=SDS((t, d), F32),
        compiler_params=_cp("parallel"),
        name="final_ln2",
    )(x, ffn_t, gate, g, b)


def _peer_q_kernel(x_ref, sc_ref, sh_ref, wt_ref, ht_ref, qt_ref):
    h = _ln(x_ref[...]) * (1.0 + sc_ref[...]) + sh_ref[...]
    ht = h.T.astype(BF16)
    ht_ref[...] = ht
    qt_ref[...] = _dot(wt_ref[...], ht)


def _peer_q_call(x, sc, sh, wqt, tm):
    t, d = x.shape
    n = wqt.shape[0]
    mod = _mod_spec(sc.shape[0], tm, d)
    return pl.pallas_call(
        _peer_q_kernel,
        grid=(t // tm,),
        in_specs=[pl.BlockSpec((tm, d), lambda i: (i, 0)), mod, mod, pl.BlockSpec((n, d), lambda i: (0, 0))],
        out_specs=[pl.BlockSpec((d, tm), lambda i: (0, i)), pl.BlockSpec((n, tm), lambda i: (0, i))],
        out_shape=[SDS((d, t), BF16), SDS((n, t), F32)],
        compiler_params=_cp("parallel"),
        name="peer_query",
    )(x, sc, sh, wqt)


def _top_rows(x, k):
    cur = x
    vals = []
    for _ in range(k):
        m = jnp.max(cur, axis=0, keepdims=True)
        vals.append(m)
        cur = jnp.where(cur == m, NEG_INF, cur)
    return vals


def _peer_sel_kernel(qt_ref, k1_ref, k2_ref, s2t_ref, thr_ref, e1t_ref, e2t_ref):
    s1 = _dot(k1_ref[...].astype(BF16), qt_ref[:N_KEYS, :].astype(BF16))
    s2 = _dot(k2_ref[...].astype(BF16), qt_ref[N_KEYS:, :].astype(BF16))
    v1 = _top_rows(s1, PEER_TOPK)
    v2 = _top_rows(s2, PEER_TOPK)
    v2s = jnp.concatenate(v2, axis=0)
    cand = jnp.concatenate([v1[a] + v2s for a in range(PEER_TOPK)], axis=0)
    top = _top_rows(cand, PEER_TOPK)
    cmax, tau = top[0], top[-1]
    sel = cand >= tau
    z = jnp.sum(jnp.where(sel, jnp.exp(cand - cmax), 0.0), axis=0, keepdims=True)
    thr = jnp.full(s1.shape, POS_INF, F32)
    for a in range(PEER_TOPK):
        grp = slice(a * PEER_TOPK, (a + 1) * PEER_TOPK)
        thr_a = jnp.min(jnp.where(cand[grp, :] >= tau, v2s, POS_INF), axis=0, keepdims=True)
        thr = jnp.where(s1 == v1[a], thr_a, thr)
    s2t_ref[...] = s2
    thr_ref[...] = thr
    e1t_ref[...] = jnp.exp(s1 - v1[0]) / z
    e2t_ref[...] = jnp.exp(s2 - v2[0])


def _peer_sel_call(qt, k1, k2, tm):
    t = qt.shape[1]
    kd = 2 * N_KEYS
    key_spec = pl.BlockSpec((N_KEYS, N_KEYS), lambda i, h: (0, 0))
    st = pl.BlockSpec((None, N_KEYS, tm), lambda i, h: (h, 0, i))
    st_shape = SDS((PEER_HEADS, N_KEYS, t), F32)
    return pl.pallas_call(
        _peer_sel_kernel,
        grid=(t // tm, PEER_HEADS),
        in_specs=[pl.BlockSpec((kd, tm), lambda i, h: (h, i)), key_spec, key_spec],
        out_specs=[st, st, st, st],
        out_shape=[st_shape] * 4,
        compiler_params=_cp("parallel", "arbitrary"),
        name="peer_select",
    )(qt, k1, k2)


GATE_ROWS = 16


def _gelu(x):
    return 0.5 * x * (1.0 + lax.erf(x * (1.0 / math.sqrt(2.0))))


def _build_gates(s2t_ref, e2t_ref, thr_rows_ref, e1_rows_ref, row0, dst_ref, r_lo, r_hi):
    tm = dst_ref.shape[1]
    for r in range(r_lo, r_hi):
        for k0 in range(0, N_KEYS, GATE_ROWS):
            ks = slice(k0, k0 + GATE_ROWS)
            w = jnp.zeros((GATE_ROWS, tm), F32)
            for h in range(PEER_HEADS):
                thr_r = thr_rows_ref[h, row0 + r:row0 + r + 1, :]
                e1_r = e1_rows_ref[h, row0 + r:row0 + r + 1, :]
                w = w + jnp.where(s2t_ref[h, ks, :] >= thr_r, e1_r * e2t_ref[h, ks, :], 0.0)
            dst_ref[r * N_KEYS + k0:r * N_KEYS + k0 + GATE_ROWS, :] = w


def _peer_dense_kernel(h2t_ref, u_ref, vt_ref, s2t_ref, e2t_ref, thr_rows_ref, e1_rows_ref, o_ref, g0_scr, g1_scr,
                       *, te):
    j = pl.program_id(1)
    keys_per_step = te // N_KEYS
    n_chunks = te // MXU_DIM
    gate_refs = (s2t_ref, e2t_ref, thr_rows_ref, e1_rows_ref)

    @pl.when(j == 0)
    def _():
        o_ref[...] = jnp.zeros_like(o_ref)
        _build_gates(*gate_refs, 0, g0_scr, 0, keys_per_step)

    def step(cur_scr, nxt_scr, nxt_row0):
        n_slices = 2 * n_chunks
        bounds = [keys_per_step * q // n_slices for q in range(n_slices + 1)]
        q = 0
        acts = []
        for c in range(n_chunks):
            rows = slice(c * MXU_DIM, (c + 1) * MXU_DIM)
            acts.append(_gelu(_dot(u_ref[rows, :], h2t_ref[...])))
            _build_gates(*gate_refs, nxt_row0, nxt_scr, bounds[q], bounds[q + 1])
            q += 1
        upd = None
        for c in range(n_chunks):
            rows = slice(c * MXU_DIM, (c + 1) * MXU_DIM)
            g = (cur_scr[rows, :] * acts[c]).astype(BF16)
            d = _dot(vt_ref[:, rows], g)
            upd = d if upd is None else upd + d
            _build_gates(*gate_refs, nxt_row0, nxt_scr, bounds[q], bounds[q + 1])
            q += 1
        o_ref[...] += upd

    @pl.when(lax.rem(j, 2) == 0)
    def _():
        step(g0_scr, g1_scr, keys_per_step)

    @pl.when(lax.rem(j, 2) == 1)
    def _():
        step(g1_scr, g0_scr, 0)


def _peer_dense_call(h2t, u_bf, vt_bf, s2t, thr, e1t, e2t, tm, te):
    d, t = h2t.shape
    ne = u_bf.shape[0]
    nj = ne // te
    keys_per_step = te // N_KEYS
    assert nj % 2 == 0 and 2 * keys_per_step == 8, "a sublane-aligned 8-row block of keys spans two expert tiles"
    st = pl.BlockSpec((PEER_HEADS, N_KEYS, tm), lambda i, j: (0, 0, i))
    nxt_rows = pl.BlockSpec((PEER_HEADS, 2 * keys_per_step, tm), lambda i, j: (0, lax.rem(j + 1, nj) // 2, i))
    return pl.pallas_call(
        functools.partial(_peer_dense_kernel, te=te),
        grid=(t // tm, nj),
        in_specs=[
            pl.BlockSpec((d, tm), lambda i, j: (0, i)),
            pl.BlockSpec((te, d), lambda i, j: (j, 0)),
            pl.BlockSpec((d, te), lambda i, j: (0, j)),
            st, st, nxt_rows, nxt_rows,
        ],
        out_specs=pl.BlockSpec((d, tm), lambda i, j: (0, i)),
        out_shape=SDS((d, t), F32),
        scratch_shapes=[pltpu.VMEM((te, tm), F32), pltpu.VMEM((te, tm), F32)],
        compiler_params=_cp("parallel", "arbitrary"),
        name="peer_dense",
    )(h2t, u_bf, vt_bf, s2t, e2t, thr, e1t)


def _rope_tables(pos):
    half = HEAD_DIM // 2
    inv = ROPE_THETA ** (-jnp.arange(half, dtype=F32) / half)
    ang = pos.astype(F32)[:, None] * inv[None, :]
    cos, sin = jnp.cos(ang), jnp.sin(ang)
    return jnp.concatenate([cos, cos], axis=1), jnp.concatenate([-sin, sin], axis=1)


def _layer_weights(l, w_in, b_f, w_o_fox, w_o_diff, w_out, peer_w_q, peer_u, peer_v):
    wl = w_in[l]
    c_fl = 3 * FOX_W
    w_r = jnp.concatenate(
        [wl[:, :c_fl], wl[:, c_fl + FOX_HEADS:], jnp.pad(wl[:, c_fl:c_fl + FOX_HEADS], ((0, 0), (0, LANES - FOX_HEADS)))],
        axis=1).astype(BF16)
    bf_pad = jnp.pad(b_f[l][None, :], ((0, 0), (0, LANES - FOX_HEADS)))
    return dict(
        w_r=w_r, bf_pad=bf_pad,
        wf=w_o_fox[l].astype(BF16), wd=w_o_diff[l].astype(BF16), wo=w_out[l].astype(BF16),
        pqt=peer_w_q[l].T.astype(BF16), u=peer_u[l].astype(BF16), vt=peer_v[l].T.astype(BF16),
    )


def _tail(x2d, mods, proj, fo, dn, lw, vecs, alpha, tm, peer_tm, peer_te):
    _, _, gate1, shift2, scale2, gate2 = mods
    t = x2d.shape[0]
    y = _mix_call(fo, dn, proj, lw["wf"], lw["wd"], tm)
    x1 = _resid_ln_call(x2d, y, gate1, lw["wo"], vecs["ln1_g"], vecs["ln1_b"], tm, alpha)
    tp = -(-t // peer_tm) * peer_tm
    pad = tp - t
    x1p = jnp.pad(x1, ((0, pad), (0, 0))) if pad else x1
    sc2, sh2 = scale2, shift2
    if pad and sc2.shape[0] != 1:
        sc2 = jnp.pad(sc2, ((0, pad), (0, 0)))
        sh2 = jnp.pad(sh2, ((0, pad), (0, 0)))
    h2t, qt = _peer_q_call(x1p, sc2, sh2, lw["pqt"], min(peer_tm, 256))
    s2t, thr, e1t, e2t = _peer_sel_call(qt, vecs["pk1"], vecs["pk2"], peer_tm)
    ffn_t = _peer_dense_call(h2t, lw["u"], lw["vt"], s2t, thr, e1t, e2t, peer_tm, peer_te)
    if pad:
        ffn_t = ffn_t[:, :t]
    return _final_ln_call(x1, ffn_t, gate2, vecs["ln2_g"], vecs["ln2_b"], tm, alpha)


def kernel(x_prompt, x_sample, cache_fox_k, cache_fox_v, cache_fox_logf, cache_diff_k, cache_diff_v, page_table, c_prompt, c_sample, w_ada, b_ada, w_in, b_f, lambda_q1, lambda_k1, lambda_q2, lambda_k2, diff_subln_w, w_o_fox, w_o_diff, w_out, ln1_g, ln1_b, peer_w_q, peer_sub_k1, peer_sub_k2, peer_u, peer_v, ln2_g, ln2_b):
    depth = w_ada.shape[0]
    bp, sp, d = x_prompt.shape
    bs, ss, _ = x_sample.shape
    assert bp == 1 and ss == 1, "one prompt sequence and one new token per sample sequence"
    n_pages = page_table.shape[1]
    page = cache_fox_k.shape[2]
    n_pool = cache_fox_k.shape[1]
    n_past = n_pages * page
    alpha = (2 * depth) ** 0.25
    dv = DIFF_V
    pg = 8

    cos_p, sin_p = _rope_tables(jnp.arange(sp))
    cos_s, sin_s = _rope_tables(n_past + jnp.arange(ss))
    lf_flat = cache_fox_logf.reshape(depth, n_pool, 1, page * FOX_HEADS)

    c_all = jnp.concatenate([c_prompt, c_sample], axis=0)
    n_c = c_all.shape[0]
    c_all = jnp.pad(c_all, ((0, -n_c % 8), (0, 0)))

    xp = x_prompt.reshape(sp, d)
    xs = x_sample.reshape(bs, d)
    st_p, st_s = [], []
    for l in range(depth):
        lam_init = 0.8 - 0.6 * math.exp(-0.3 * l)
        lw = _layer_weights(l, w_in, b_f, w_o_fox, w_o_diff, w_out, peer_w_q, peer_u, peer_v)
        vecs = dict(ln1_g=ln1_g[l][None], ln1_b=ln1_b[l][None], ln2_g=ln2_g[l][None], ln2_b=ln2_b[l][None],
                    pk1=peer_sub_k1[l], pk2=peer_sub_k2[l])
        lam_vecs = [v[l][None] for v in (lambda_q1, lambda_k1, lambda_q2, lambda_k2)]
        subln = diff_subln_w[l][None]
        subln_rep = jnp.broadcast_to(diff_subln_w[l][:, None], (dv, LANES))

        mod = _ada_call(c_all, w_ada, b_ada, l)
        mods_p = [mod[0:bp, i * d:(i + 1) * d] for i in range(6)]
        mods_s = [mod[bp:bp + bs, i * d:(i + 1) * d] for i in range(6)]

        proj = _lnmod_mm_call(xp, mods_p[1], mods_p[0], lw["w_r"], 512, 1152)
        (fk_o, fv_o, dk_o, dv_o, logf, fqt, fkb, fvt, dqt, dkb, dvt, ckb, cumt) = _post_prompt_call(
            proj, cos_p, sin_p, lw["bf_pad"], 256)
        fo = _fox_flash_call(fqt, fkb, fvt, cumt, ckb, 512)
        dn = _diff_flash_call(dqt, dkb, dvt, lam_vecs, subln_rep, 512, lam_init)
        st_p.append((fk_o.reshape(bp, sp, FOX_HEADS, HEAD_DIM), fv_o.reshape(bp, sp, FOX_HEADS, HEAD_DIM),
                     logf.reshape(bp, sp, FOX_HEADS),
                     dk_o.reshape(bp, sp, DIFF_HEADS, dv), dv_o.reshape(bp, sp, DIFF_HEADS, dv)))
        xp = _tail(xp, mods_p, proj, fo, dn, lw, vecs, alpha, 256, 512, 512)

        proj_s = _lnmod_mm_call(xs, mods_s[1], mods_s[0], lw["w_r"], bs, 1152)
        dq_s, dk_s, logf_s = _post_call(proj_s, cos_s, sin_s, lw["bf_pad"], bs)
        fk_s = proj_s[:, FOX_W:2 * FOX_W].reshape(bs, FOX_HEADS, HEAD_DIM)
        fv_s = proj_s[:, 2 * FOX_W:3 * FOX_W].reshape(bs, FOX_HEADS, HEAD_DIM)
        dv_s = proj_s[:, 5 * FOX_W:6 * FOX_W].reshape(bs, DIFF_HEADS, dv)
        lfn_cls = jnp.tile(logf_s, (1, LANES // FOX_HEADS)).reshape(bs, 1, LANES)
        fo_s = _fox_dec_call(l, page_table, proj_s[:, :FOX_W].reshape(bs, FOX_HEADS, HEAD_DIM), fk_s, fv_s, lfn_cls,
                             cache_fox_k, cache_fox_v, lf_flat, pg)
        dn_s = _diff_dec_call(l, page_table, dq_s.reshape(bs, N_MAPS, HEAD_DIM), dk_s.reshape(bs, N_MAPS, HEAD_DIM),
                              jnp.repeat(dv_s, 2, axis=1), lam_vecs, subln, cache_diff_k, cache_diff_v, pg, lam_init)
        dn_s = dn_s[:, ::2, :].reshape(bs, FOX_W)
        st_s.append((fk_s.reshape(bs, ss, FOX_HEADS, HEAD_DIM), fv_s.reshape(bs, ss, FOX_HEADS, HEAD_DIM),
                     logf_s.reshape(bs, ss, FOX_HEADS), dk_s.reshape(bs, ss, DIFF_HEADS, dv),
                     dv_s.reshape(bs, ss, DIFF_HEADS, dv)))
        xs = _tail(xs, mods_s, proj_s, fo_s.reshape(bs, FOX_W), dn_s, lw, vecs, alpha, bs, LANES, 512)

    outs_p = [jnp.stack([s[i] for s in st_p]) for i in range(5)]
    outs_s = [jnp.stack([s[i] for s in st_s]) for i in range(5)]
    return (xp.reshape(bp, sp, d), xs.reshape(bs, ss, d), *outs_p, *outs_s)
```

```python
import functools
import math

import jax
import jax.numpy as jnp
import numpy as np
from jax import lax
from jax.experimental import pallas as pl
from jax.experimental.pallas import tpu as pltpu

F32 = jnp.float32
BF16 = jnp.bfloat16
NEG_INF = float("-inf")
POS_INF = float("inf")

LN_EPS = 1e-5
ROPE_THETA = 10000.0
HEAD_DIM = 128
FOX_HEADS = 8
DIFF_HEADS = 4
N_MAPS = 8
FOX_W = FOX_HEADS * HEAD_DIM
DIFF_V = 2 * HEAD_DIM
PEER_HEADS = 8
PEER_TOPK = 16
N_KEYS = 128
ATTN_SCALE = HEAD_DIM ** -0.5
LOG2E = 1.0 / math.log(2.0)
QK_SCALE_LOG2 = ATTN_SCALE * LOG2E
LANES = 128
MXU_DIM = 256
VMEM_LIMIT = 48 * 1024 * 1024

SDS = jax.ShapeDtypeStruct


def _cp(*sem):
    return pltpu.CompilerParams(dimension_semantics=sem, vmem_limit_bytes=VMEM_LIMIT)


def _ln(x):
    mu = jnp.mean(x, axis=-1, keepdims=True)
    xc = x - mu
    var = jnp.mean(xc * xc, axis=-1, keepdims=True)
    return xc * lax.rsqrt(var + LN_EPS)


def _dot_nt(a, b):
    return lax.dot_general(a, b, (((1,), (1,)), ((), ())), preferred_element_type=F32)


def _dot(a, b):
    return jnp.dot(a, b, preferred_element_type=F32)


def _rep_chunks(x, k):
    return x if k == 1 else jnp.concatenate([x] * k, axis=1)


def _ada_kernel(c_ref, w_ref, b_ref, o_ref):
    c = c_ref[...]
    a = (c * jax.nn.sigmoid(c)).astype(BF16)
    o_ref[...] = _dot(a, w_ref[...].astype(BF16)) + b_ref[...]


def _ada_call(c_all, w_ada, b_ada, layer):
    mp, d = c_all.shape
    n = w_ada.shape[2]
    tn = 1024
    return pl.pallas_call(
        _ada_kernel,
        grid=(n // tn,),
        in_specs=[
            pl.BlockSpec((mp, d), lambda j: (0, 0)),
            pl.BlockSpec((None, d, tn), lambda j: (layer, 0, j)),
            pl.BlockSpec((None, 1, tn), lambda j: (layer, 0, j)),
        ],
        out_specs=pl.BlockSpec((mp, tn), lambda j: (0, j)),
        out_shape=SDS((mp, n), F32),
        compiler_params=_cp("arbitrary"),
        name="ada",
    )(c_all, w_ada, b_ada.reshape(b_ada.shape[0], 1, n))


def _lnmod_mm_kernel(x_ref, sc_ref, sh_ref, w_ref, o_ref, h_scr):
    @pl.when(pl.program_id(1) == 0)
    def _():
        h = _ln(x_ref[...]) * (1.0 + sc_ref[...]) + sh_ref[...]
        h_scr[...] = h.astype(BF16)

    o_ref[...] = _dot(h_scr[...], w_ref[...])


def _mod_spec2(r, tm, d):
    return pl.BlockSpec((1, d), lambda i, j: (0, 0)) if r == 1 else pl.BlockSpec((tm, d), lambda i, j: (i, 0))


def _lnmod_mm_call(x, sc, sh, w, tm, tn):
    t, d = x.shape
    n = w.shape[1]
    mod_spec = _mod_spec2(sc.shape[0], tm, d)
    return pl.pallas_call(
        _lnmod_mm_kernel,
        grid=(t // tm, n // tn),
        in_specs=[
            pl.BlockSpec((tm, d), lambda i, j: (i, 0)),
            mod_spec,
            mod_spec,
            pl.BlockSpec((d, tn), lambda i, j: (0, j)),
        ],
        out_specs=pl.BlockSpec((tm, tn), lambda i, j: (i, j)),
        out_shape=SDS((t, n), F32),
        scratch_shapes=[pltpu.VMEM((tm, d), BF16)],
        compiler_params=_cp("parallel", "arbitrary"),
        name="lnmod_mm",
    )(x, sc, sh, w)


def _rope(x, c, s):
    return x * c + pltpu.roll(x, HEAD_DIM // 2, axis=1) * s


def _log_sigmoid(z):
    return jnp.minimum(z, 0.0) - jnp.log1p(jnp.exp(-jnp.abs(z)))


def _post_kernel(dq_ref, dk_ref, fl_ref, cos_ref, sin_ref, bf_ref, dqo_ref, dko_ref, lf_ref):
    c = cos_ref[...]
    s = sin_ref[...]
    for j in range(N_MAPS):
        sl = slice(j * HEAD_DIM, (j + 1) * HEAD_DIM)
        dqo_ref[:, sl] = _rope(dq_ref[:, sl], c, s)
        dko_ref[:, sl] = _rope(dk_ref[:, sl], c, s)
    lf_ref[...] = _log_sigmoid(fl_ref[...] + bf_ref[...])[:, :FOX_HEADS]


def _post_call(proj, cos2, sin2, bf_pad, tm):
    t = proj.shape[0]
    tab_spec = pl.BlockSpec((1, LANES), lambda i: (0, 0))
    row = pl.BlockSpec((tm, FOX_W), lambda i: (i, 0))
    return pl.pallas_call(
        _post_kernel,
        grid=(t // tm,),
        in_specs=[
            pl.BlockSpec((tm, FOX_W), lambda i: (i, 3)),
            pl.BlockSpec((tm, FOX_W), lambda i: (i, 4)),
            pl.BlockSpec((tm, LANES), lambda i: (i, 80)),
            tab_spec, tab_spec, tab_spec,
        ],
        out_specs=[row, row, pl.BlockSpec((tm, FOX_HEADS), lambda i: (i, 0))],
        out_shape=[SDS((t, FOX_W), F32), SDS((t, FOX_W), F32), SDS((t, FOX_HEADS), F32)],
        compiler_params=_cp("arbitrary"),
        name="post_proj",
    )(proj, proj, proj, cos2, sin2, bf_pad)


def _post_prompt_kernel(fq_ref, fk_ref, fv_ref, dq_ref, dk_ref, dv_ref, fl_ref, cos_ref, sin_ref, bf_ref,
                        fko_ref, fvo_ref, dko_ref, dvo_ref, lf_ref,
                        fqt_ref, fkb_ref, fvt_ref, dqt_ref, dkb_ref, dvt_ref, ckb_ref, cumt_ref,
                        carry_scr, *, tm):
    c = cos_ref[...]
    s = sin_ref[...]
    for j in range(N_MAPS):
        sl = slice(j * HEAD_DIM, (j + 1) * HEAD_DIM)
        half = slice((j % 2) * HEAD_DIM, (j % 2 + 1) * HEAD_DIM)
        dk = _rope(dk_ref[:, sl], c, s)
        dko_ref[:, j // 2, half] = dk
        dkb_ref[:, sl] = dk.astype(BF16)
        dqt_ref[sl, :] = (_rope(dq_ref[:, sl], c, s) * QK_SCALE_LOG2).T.astype(BF16)
        fqt_ref[sl, :] = (fq_ref[:, sl] * QK_SCALE_LOG2).T.astype(BF16)
        fk = fk_ref[:, sl]
        fko_ref[:, j, :] = fk
        fkb_ref[:, sl] = fk.astype(BF16)
        fv = fv_ref[:, sl]
        fvo_ref[:, j, :] = fv
        fvt_ref[sl, :] = fv.T.astype(BF16)
        dv = dv_ref[:, sl]
        dvo_ref[:, j // 2, half] = dv
        dvt_ref[sl, :] = dv.T.astype(BF16)
    lf = _log_sigmoid(fl_ref[...] + bf_ref[...])
    lf_ref[...] = lf[:, :FOX_HEADS]

    @pl.when(pl.program_id(0) == 0)
    def _():
        carry_scr[...] = jnp.zeros_like(carry_scr)

    row = lax.broadcasted_iota(jnp.int32, (tm, tm), 0)
    col = lax.broadcasted_iota(jnp.int32, (tm, tm), 1)
    tri = jnp.where(col <= row, 1.0, 0.0).astype(BF16)
    hi = lf.astype(BF16)
    r1 = lf - hi.astype(F32)
    mid = r1.astype(BF16)
    lo = (r1 - mid.astype(F32)).astype(BF16)
    cum = (_dot(tri, hi) + _dot(tri, mid) + _dot(tri, lo)) + carry_scr[...]
    carry_scr[...] = cum[tm - 1:tm, :]
    cum2 = cum * LOG2E
    cumt_ref[...] = cum2.T[:FOX_HEADS, :]
    for h in range(FOX_HEADS):
        ckb_ref[h] = jnp.broadcast_to(cum2[:, h:h + 1], (tm, LANES))


def _post_prompt_call(proj, cos2, sin2, bf_pad, tm):
    t = proj.shape[0]
    col = lambda c: pl.BlockSpec((tm, FOX_W), lambda i: (i, c))
    tab = pl.BlockSpec((tm, LANES), lambda i: (i, 0))
    row_f32 = pl.BlockSpec((tm, FOX_W), lambda i: (i, 0))
    tr = pl.BlockSpec((FOX_W, tm), lambda i: (0, i))
    fox_st = pl.BlockSpec((tm, FOX_HEADS, HEAD_DIM), lambda i: (i, 0, 0))
    diff_st = pl.BlockSpec((tm, DIFF_HEADS, DIFF_V), lambda i: (i, 0, 0))
    return pl.pallas_call(
        functools.partial(_post_prompt_kernel, tm=tm),
        grid=(t // tm,),
        in_specs=[col(0), col(1), col(2), col(3), col(4), col(5),
                  pl.BlockSpec((tm, LANES), lambda i: (i, 80)), tab, tab,
                  pl.BlockSpec((1, LANES), lambda i: (0, 0))],
        out_specs=[fox_st, fox_st, diff_st, diff_st, pl.BlockSpec((tm, FOX_HEADS), lambda i: (i, 0)),
                   tr, row_f32, tr, tr, row_f32, tr,
                   pl.BlockSpec((FOX_HEADS, tm, LANES), lambda i: (0, i, 0)),
                   pl.BlockSpec((FOX_HEADS, tm), lambda i: (0, i))],
        out_shape=[SDS((t, FOX_HEADS, HEAD_DIM), F32), SDS((t, FOX_HEADS, HEAD_DIM), F32),
                   SDS((t, DIFF_HEADS, DIFF_V), F32), SDS((t, DIFF_HEADS, DIFF_V), F32), SDS((t, FOX_HEADS), F32),
                   SDS((FOX_W, t), BF16), SDS((t, FOX_W), BF16), SDS((FOX_W, t), BF16),
                   SDS((FOX_W, t), BF16), SDS((t, FOX_W), BF16), SDS((FOX_W, t), BF16),
                   SDS((FOX_HEADS, t, LANES), F32), SDS((FOX_HEADS, t), F32)],
        scratch_shapes=[pltpu.VMEM((1, LANES), F32)],
        compiler_params=_cp("arbitrary"),
        name="post_prompt",
    )(proj, proj, proj, proj, proj, proj, proj, cos2, sin2, bf_pad)


def _tri_pairs(n):
    qi = np.concatenate([np.full(i + 1, i, np.int32) for i in range(n)])
    ki = np.concatenate([np.arange(i + 1, dtype=np.int32) for i in range(n)])
    return jnp.asarray(qi), jnp.asarray(ki)


def _flash_update(st, vt_bf, m_scr, l_scr, acc_scr, idx, rows):
    m_prev = m_scr[idx:idx + 1, :]
    m_new = jnp.maximum(m_prev, jnp.max(st, axis=0, keepdims=True))
    alpha = jnp.exp2(m_prev - m_new)
    p = jnp.exp2(st - m_new)
    l_scr[idx:idx + 1, :] = alpha * l_scr[idx:idx + 1, :] + jnp.sum(p, axis=0, keepdims=True)
    acc_scr[rows, :] = alpha * acc_scr[rows, :] + _dot(vt_bf, p.astype(BF16))
    m_scr[idx:idx + 1, :] = m_new


def _flash_init(m_scr, l_scr, acc_scr):
    m_scr[...] = jnp.full_like(m_scr, NEG_INF)
    l_scr[...] = jnp.zeros_like(l_scr)
    acc_scr[...] = jnp.zeros_like(acc_scr)


def _causal_keep_t(tk, tq):
    kpos = lax.broadcasted_iota(jnp.int32, (tk, tq), 0)
    qpos = lax.broadcasted_iota(jnp.int32, (tk, tq), 1)
    return kpos <= qpos


def _fox_flash_kernel(qi_tab, ki_tab, qt_ref, k_ref, vt_ref, cq_ref, ckb_ref, o_ref, m_scr, l_scr, acc_scr, *, tq, tk):
    pid = pl.program_id(0)
    qi = qi_tab[pid]
    ki = ki_tab[pid]

    @pl.when(ki == 0)
    def _():
        _flash_init(m_scr, l_scr, acc_scr)

    def body(masked):
        keep = _causal_keep_t(tk, tq) if masked else None
        for h in range(FOX_HEADS):
            rows = slice(h * HEAD_DIM, (h + 1) * HEAD_DIM)
            st = _dot(k_ref[:, rows], qt_ref[rows, :])
            st = st + (cq_ref[h:h + 1, :] - _rep_chunks(ckb_ref[h], tq // LANES))
            if masked:
                st = jnp.where(keep, st, NEG_INF)
            _flash_update(st, vt_ref[rows, :], m_scr, l_scr, acc_scr, h, rows)

    @pl.when(ki < qi)
    def _():
        body(False)

    @pl.when(ki == qi)
    def _():
        body(True)
        for h in range(FOX_HEADS):
            rows = slice(h * HEAD_DIM, (h + 1) * HEAD_DIM)
            o_ref[:, rows] = (acc_scr[rows, :] / l_scr[h:h + 1, :]).T.astype(o_ref.dtype)


def _fox_flash_call(fqt, fkb, fvt, cumt, ckb, tq):
    t = fkb.shape[0]
    nq = t // tq
    qi_tab, ki_tab = _tri_pairs(nq)
    grid_spec = pltpu.PrefetchScalarGridSpec(
        num_scalar_prefetch=2,
        grid=(int(qi_tab.shape[0]),),
        in_specs=[
            pl.BlockSpec((FOX_W, tq), lambda p, qt, kt: (0, qt[p])),
            pl.BlockSpec((tq, FOX_W), lambda p, qt, kt: (kt[p], 0)),
            pl.BlockSpec((FOX_W, tq), lambda p, qt, kt: (0, kt[p])),
            pl.BlockSpec((FOX_HEADS, tq), lambda p, qt, kt: (0, qt[p])),
            pl.BlockSpec((FOX_HEADS, tq, LANES), lambda p, qt, kt: (0, kt[p], 0)),
        ],
        out_specs=pl.BlockSpec((tq, FOX_W), lambda p, qt, kt: (qt[p], 0)),
        scratch_shapes=[
            pltpu.VMEM((N_MAPS, tq), F32),
            pltpu.VMEM((N_MAPS, tq), F32),
            pltpu.VMEM((FOX_W, tq), F32),
        ],
    )
    return pl.pallas_call(
        functools.partial(_fox_flash_kernel, tq=tq, tk=tq),
        grid_spec=grid_spec,
        out_shape=SDS((t, FOX_W), BF16),
        compiler_params=_cp("arbitrary"),
        name="fox_flash",
    )(qi_tab, ki_tab, fqt, fkb, fvt, cumt, ckb)


def _lambda_full(lq1_ref, lk1_ref, lq2_ref, lk2_ref, lam_init):
    a = jnp.exp(jnp.sum(lq1_ref[...] * lk1_ref[...], axis=1, keepdims=True))
    b = jnp.exp(jnp.sum(lq2_ref[...] * lk2_ref[...], axis=1, keepdims=True))
    return a - b + lam_init


def _diff_flash_kernel(qi_tab, ki_tab, qt_ref, k_ref, vt_ref, lq1_ref, lk1_ref, lq2_ref, lk2_ref, sub_ref,
                       o_ref, m_scr, l_scr, acc_scr, *, tq, tk, lam_init):
    pid = pl.program_id(0)
    qi = qi_tab[pid]
    ki = ki_tab[pid]

    @pl.when(ki == 0)
    def _():
        _flash_init(m_scr, l_scr, acc_scr)

    def body(masked):
        keep = _causal_keep_t(tk, tq) if masked else None
        for mp in range(N_MAPS):
            hd = mp // 2
            rows = slice(mp * HEAD_DIM, (mp + 1) * HEAD_DIM)
            st = _dot(k_ref[:, rows], qt_ref[rows, :])
            if masked:
                st = jnp.where(keep, st, NEG_INF)
            _flash_update(st, vt_ref[hd * DIFF_V:(hd + 1) * DIFF_V, :], m_scr, l_scr, acc_scr, mp,
                          slice(mp * DIFF_V, (mp + 1) * DIFF_V))

    @pl.when(ki < qi)
    def _():
        body(False)

    @pl.when(ki == qi)
    def _():
        body(True)
        lam = _lambda_full(lq1_ref, lk1_ref, lq2_ref, lk2_ref, lam_init)
        sub = _rep_chunks(sub_ref[...], tq // LANES)
        for hd in range(DIFF_HEADS):
            o1 = acc_scr[(2 * hd) * DIFF_V:(2 * hd + 1) * DIFF_V, :] / l_scr[2 * hd:2 * hd + 1, :]
            o2 = acc_scr[(2 * hd + 1) * DIFF_V:(2 * hd + 2) * DIFF_V, :] / l_scr[2 * hd + 1:2 * hd + 2, :]
            d = o1 - lam * o2
            y = d * lax.rsqrt(jnp.mean(d * d, axis=0, keepdims=True) + LN_EPS)
            o_ref[:, hd * DIFF_V:(hd + 1) * DIFF_V] = ((y * sub) * (1.0 - lam_init)).T.astype(o_ref.dtype)


def _diff_flash_call(dqt, dkb, dvt, lam_vecs, subln_rep, tq, lam_init):
    t = dkb.shape[0]
    nq = t // tq
    qi_tab, ki_tab = _tri_pairs(nq)
    vec = pl.BlockSpec((1, HEAD_DIM), lambda p, qt, kt: (0, 0))
    grid_spec = pltpu.PrefetchScalarGridSpec(
        num_scalar_prefetch=2,
        grid=(int(qi_tab.shape[0]),),
        in_specs=[
            pl.BlockSpec((FOX_W, tq), lambda p, qt, kt: (0, qt[p])),
            pl.BlockSpec((tq, FOX_W), lambda p, qt, kt: (kt[p], 0)),
            pl.BlockSpec((FOX_W, tq), lambda p, qt, kt: (0, kt[p])),
            vec, vec, vec, vec,
            pl.BlockSpec((DIFF_V, LANES), lambda p, qt, kt: (0, 0)),
        ],
        out_specs=pl.BlockSpec((tq, FOX_W), lambda p, qt, kt: (qt[p], 0)),
        scratch_shapes=[
            pltpu.VMEM((N_MAPS, tq), F32),
            pltpu.VMEM((N_MAPS, tq), F32),
            pltpu.VMEM((N_MAPS * DIFF_V, tq), F32),
        ],
    )
    return pl.pallas_call(
        functools.partial(_diff_flash_kernel, tq=tq, tk=tq, lam_init=lam_init),
        grid_spec=grid_spec,
        out_shape=SDS((t, FOX_W), BF16),
        compiler_params=_cp("arbitrary"),
        name="diff_flash",
    )(qi_tab, ki_tab, dqt, dkb, dvt, *lam_vecs, subln_rep)


def _class_allreduce(x, op, period):
    s = period
    while s < LANES:
        x = op(x, pltpu.roll(x, s, axis=1))
        s *= 2
    return x


def _fold_chunks(x, op):
    out = x[:, :LANES]
    for c in range(1, x.shape[1] // LANES):
        out = op(out, x[:, c * LANES:(c + 1) * LANES])
    return out


def _suffix_incl(x, period):
    n = x.shape[1]
    lane = lax.broadcasted_iota(jnp.int32, x.shape, 1)
    s = period
    while s < n:
        y = pltpu.roll(x, n - s, axis=1)
        x = x + jnp.where(lane < n - s, y, 0.0)
        s *= 2
    return x


def _fox_dec_kernel(pt_ref, q_ref, kn_ref, vn_ref, lfn_ref, *refs, pg, nj):
    del pt_ref
    k_refs, v_refs, lf_refs = refs[:pg], refs[pg:2 * pg], refs[2 * pg:3 * pg]
    o_ref = refs[3 * pg]
    m_scr, l_scr, acc_scr, carry_scr = refs[3 * pg + 1:]
    j = pl.program_id(1)
    nh = FOX_HEADS
    w = PAGE_ROWS_FOX
    sub = lax.broadcasted_iota(jnp.int32, (nh, w), 0)
    lane = lax.broadcasted_iota(jnp.int32, (nh, w), 1)
    diag = sub == (lane & (nh - 1))
    diag1 = diag[:, :LANES]

    def to_sub(x_cls):
        return jnp.max(jnp.where(diag1, jnp.broadcast_to(x_cls, (nh, LANES)), NEG_INF), axis=1, keepdims=True)

    @pl.when(j == 0)
    def _():
        m_scr[...] = jnp.full_like(m_scr, NEG_INF)
        l_scr[...] = jnp.zeros_like(l_scr)
        acc_scr[...] = jnp.zeros_like(acc_scr)
        carry_scr[...] = lfn_ref[...]

    qb = (q_ref[...] * ATTN_SCALE).astype(BF16)

    lf = jnp.concatenate([r[...] for r in lf_refs], axis=0)
    tot = _class_allreduce(_fold_chunks(lf, jnp.add), jnp.add, nh)
    run = carry_scr[...]
    before = []
    for i in range(pg):
        before.append(run)
        run = run + tot[i:i + 1, :]
    carry_scr[...] = run
    bias = (_suffix_incl(lf, nh) - lf) + _rep_chunks(jnp.concatenate(before, axis=0), w // LANES)

    rows = []
    for i in range(pg):
        k2 = k_refs[i][...].reshape(w, HEAD_DIM).astype(BF16)
        st = _dot_nt(qb, k2)
        rows.append(jnp.sum(jnp.where(diag, st, 0.0), axis=0, keepdims=True))
    s = jnp.concatenate(rows, axis=0) + bias

    mc = _class_allreduce(_fold_chunks(jnp.max(s, axis=0, keepdims=True), jnp.maximum), jnp.maximum, nh)
    m_old = m_scr[...]
    m_new = jnp.maximum(m_old, mc)
    alpha = jnp.exp(m_old - m_new)
    p = jnp.exp(s - _rep_chunks(m_new, w // LANES))
    l_scr[...] = l_scr[...] * _rep_chunks(alpha, w // LANES) + p

    acc_c = jnp.zeros((nh, HEAD_DIM), F32)
    for i in range(pg):
        pm = jnp.where(diag, jnp.broadcast_to(p[i:i + 1, :], (nh, w)), 0.0).astype(BF16)
        v2 = v_refs[i][...].reshape(w, HEAD_DIM).astype(BF16)
        acc_c = acc_c + _dot(pm, v2)
    acc_scr[...] = acc_scr[...] * to_sub(alpha) + acc_c
    m_scr[...] = m_new

    @pl.when(j == nj - 1)
    def _():
        kn = kn_ref[...].astype(BF16).astype(F32)
        s_self = jnp.sum(qb.astype(F32) * kn, axis=1, keepdims=True)
        s_cls = jnp.sum(jnp.where(diag1, jnp.broadcast_to(s_self, (nh, LANES)), 0.0), axis=0, keepdims=True)
        m_fin = jnp.maximum(m_new, s_cls)
        a2 = jnp.exp(m_new - m_fin)
        p_self = jnp.exp(s_cls - m_fin)
        l_cls = _class_allreduce(_fold_chunks(jnp.sum(l_scr[...], axis=0, keepdims=True), jnp.add), jnp.add, nh)
        l_cls = l_cls * a2 + p_self
        o = (acc_scr[...] * to_sub(a2) + to_sub(p_self) * vn_ref[...]) / to_sub(l_cls)
        o_ref[...] = o


PAGE_ROWS_FOX = 128 * FOX_HEADS
PAGE_ROWS_DIFF = 128 * DIFF_HEADS


def _fox_dec_call(layer, page_table, q, k_new, v_new, lfn_cls, cache_k, cache_v, cache_lf_flat, pg):
    b = q.shape[0]
    n_pages = page_table.shape[1]
    page = cache_k.shape[2]
    nj = n_pages // pg

    def page_idx(i):
        return lambda bb, j, pt: (layer, pt[bb, n_pages - 1 - (j * pg + i)], 0, 0, 0)

    def lf_idx(i):
        return lambda bb, j, pt: (layer, pt[bb, n_pages - 1 - (j * pg + i)], 0, 0)

    tok = pl.BlockSpec((None, FOX_HEADS, HEAD_DIM), lambda bb, j, pt: (bb, 0, 0))
    in_specs = [tok, tok, tok, pl.BlockSpec((None, 1, LANES), lambda bb, j, pt: (bb, 0, 0))]
    in_specs += [pl.BlockSpec((None, None, page, FOX_HEADS, HEAD_DIM), page_idx(i)) for i in range(pg)]
    in_specs += [pl.BlockSpec((None, None, page, FOX_HEADS, HEAD_DIM), page_idx(i)) for i in range(pg)]
    in_specs += [pl.BlockSpec((None, None, 1, page * FOX_HEADS), lf_idx(i)) for i in range(pg)]
    grid_spec = pltpu.PrefetchScalarGridSpec(
        num_scalar_prefetch=1,
        grid=(b, nj),
        in_specs=in_specs,
        out_specs=pl.BlockSpec((None, FOX_HEADS, HEAD_DIM), lambda bb, j, pt: (bb, 0, 0)),
        scratch_shapes=[
            pltpu.VMEM((1, LANES), F32),
            pltpu.VMEM((pg, page * FOX_HEADS), F32),
            pltpu.VMEM((FOX_HEADS, HEAD_DIM), F32),
            pltpu.VMEM((1, LANES), F32),
        ],
    )
    return pl.pallas_call(
        functools.partial(_fox_dec_kernel, pg=pg, nj=nj),
        grid_spec=grid_spec,
        out_shape=SDS((b, FOX_HEADS, HEAD_DIM), F32),
        compiler_params=_cp("parallel", "arbitrary"),
        name="fox_decode",
    )(page_table, q, k_new, v_new, lfn_cls, *([cache_k] * pg), *([cache_v] * pg), *([cache_lf_flat] * pg))


def _diff_dec_kernel(pt_ref, q_ref, kn_ref, vn_ref, lq1_ref, lk1_ref, lq2_ref, lk2_ref, sub_ref, *refs,
                     pg, nj, lam_init):
    del pt_ref
    k_refs, v_refs = refs[:pg], refs[pg:2 * pg]
    o_ref = refs[2 * pg]
    m_scr, l_scr, acc_scr = refs[2 * pg + 1:]
    j = pl.program_id(1)
    nh = DIFF_HEADS
    w = PAGE_ROWS_DIFF
    dv = DIFF_V
    sub = lax.broadcasted_iota(jnp.int32, (N_MAPS, w), 0)
    lane = lax.broadcasted_iota(jnp.int32, (N_MAPS, w), 1)
    same_head = (sub >> 1) == (lane & (nh - 1))
    sel = [jnp.logical_and(same_head, (sub & 1) == n) for n in range(2)]
    sel1 = [x[:, :LANES] for x in sel]

    def to_sub(x0, x1):
        y = jnp.where(sel1[0], jnp.broadcast_to(x0, (N_MAPS, LANES)),
                      jnp.where(sel1[1], jnp.broadcast_to(x1, (N_MAPS, LANES)), NEG_INF))
        return jnp.max(y, axis=1, keepdims=True)

    @pl.when(j == 0)
    def _():
        m_scr[...] = jnp.full_like(m_scr, NEG_INF)
        l_scr[...] = jnp.zeros_like(l_scr)
        acc_scr[...] = jnp.zeros_like(acc_scr)

    q8 = q_ref[...] * ATTN_SCALE
    odd = (lax.broadcasted_iota(jnp.int32, (N_MAPS, HEAD_DIM), 0) & 1) == 1
    qd = jnp.concatenate([jnp.where(odd, 0.0, q8), jnp.where(odd, q8, 0.0)], axis=1).astype(BF16)

    rows = [[], []]
    for i in range(pg):
        k2 = k_refs[i][...].reshape(w, dv).astype(BF16)
        st = _dot_nt(qd, k2)
        for n in range(2):
            rows[n].append(jnp.sum(jnp.where(sel[n], st, 0.0), axis=0, keepdims=True))
    s = [jnp.concatenate(rows[n], axis=0) for n in range(2)]

    p, alpha = [], []
    for n in range(2):
        mc = _class_allreduce(_fold_chunks(jnp.max(s[n], axis=0, keepdims=True), jnp.maximum), jnp.maximum, nh)
        m_old = m_scr[n]
        m_new = jnp.maximum(m_old, mc)
        a = jnp.exp(m_old - m_new)
        pn = jnp.exp(s[n] - _rep_chunks(m_new, w // LANES))
        l_scr[n] = l_scr[n] * _rep_chunks(a, w // LANES) + pn
        m_scr[n] = m_new
        p.append(pn)
        alpha.append(a)

    acc_c = jnp.zeros((N_MAPS, dv), F32)
    for i in range(pg):
        pm = jnp.where(sel[0], jnp.broadcast_to(p[0][i:i + 1, :], (N_MAPS, w)),
                       jnp.where(sel[1], jnp.broadcast_to(p[1][i:i + 1, :], (N_MAPS, w)), 0.0)).astype(BF16)
        v2 = v_refs[i][...].reshape(w, dv).astype(BF16)
        acc_c = acc_c + _dot(pm, v2)
    acc_scr[...] = acc_scr[...] * to_sub(alpha[0], alpha[1]) + acc_c

    @pl.when(j == nj - 1)
    def _():
        kn = kn_ref[...].astype(BF16).astype(F32)
        s_self = jnp.sum(q8.astype(BF16).astype(F32) * kn, axis=1, keepdims=True)
        a2, p_self, l_cls = [], [], []
        for n in range(2):
            s_cls = jnp.sum(jnp.where(sel1[n], jnp.broadcast_to(s_self, (N_MAPS, LANES)), 0.0),
                            axis=0, keepdims=True)
            m_new = m_scr[n]
            m_fin = jnp.maximum(m_new, s_cls)
            a = jnp.exp(m_new - m_fin)
            ps = jnp.exp(s_cls - m_fin)
            lc = _class_allreduce(_fold_chunks(jnp.sum(l_scr[n], axis=0, keepdims=True), jnp.add), jnp.add, nh)
            a2.append(a)
            p_self.append(ps)
            l_cls.append(lc * a + ps)
        o = (acc_scr[...] * to_sub(a2[0], a2[1]) + to_sub(p_self[0], p_self[1]) * vn_ref[...]) / to_sub(l_cls[0], l_cls[1])
        lam = _lambda_full(lq1_ref, lk1_ref, lq2_ref, lk2_ref, lam_init)
        d = o - lam * pltpu.roll(o, N_MAPS - 1, axis=0)
        y = d * lax.rsqrt(jnp.mean(d * d, axis=1, keepdims=True) + LN_EPS)
        o_ref[...] = (y * sub_ref[...]) * (1.0 - lam_init)


def _diff_dec_call(layer, page_table, q8, k_new8, v_new8, lam_vecs, subln, cache_k, cache_v, pg, lam_init):
    b = q8.shape[0]
    n_pages = page_table.shape[1]
    page = cache_k.shape[2]
    nj = n_pages // pg
    dv = DIFF_V

    def page_idx(i):
        return lambda bb, j, pt: (layer, pt[bb, n_pages - 1 - (j * pg + i)], 0, 0, 0)

    vec = pl.BlockSpec((1, HEAD_DIM), lambda bb, j, pt: (0, 0))
    tok = pl.BlockSpec((None, N_MAPS, HEAD_DIM), lambda bb, j, pt: (bb, 0, 0))
    in_specs = [tok, tok, pl.BlockSpec((None, N_MAPS, dv), lambda bb, j, pt: (bb, 0, 0)),
                vec, vec, vec, vec, pl.BlockSpec((1, dv), lambda bb, j, pt: (0, 0))]
    in_specs += [pl.BlockSpec((None, None, page, DIFF_HEADS, dv), page_idx(i)) for i in range(pg)]
    in_specs += [pl.BlockSpec((None, None, page, DIFF_HEADS, dv), page_idx(i)) for i in range(pg)]
    grid_spec = pltpu.PrefetchScalarGridSpec(
        num_scalar_prefetch=1,
        grid=(b, nj),
        in_specs=in_specs,
        out_specs=pl.BlockSpec((None, N_MAPS, dv), lambda bb, j, pt: (bb, 0, 0)),
        scratch_shapes=[
            pltpu.VMEM((2, 1, LANES), F32),
            pltpu.VMEM((2, pg, page * DIFF_HEADS), F32),
            pltpu.VMEM((N_MAPS, dv), F32),
        ],
    )
    return pl.pallas_call(
        functools.partial(_diff_dec_kernel, pg=pg, nj=nj, lam_init=lam_init),
        grid_spec=grid_spec,
        out_shape=SDS((b, N_MAPS, dv), F32),
        compiler_params=_cp("parallel", "arbitrary"),
        name="diff_decode",
    )(page_table, q8, k_new8, v_new8, *lam_vecs, subln, *([cache_k] * pg), *([cache_v] * pg))


def _mix_kernel(fo_ref, dn_ref, ga_ref, gb_ref, wf_ref, wd_ref, o_ref):
    yf = _dot(fo_ref[...].astype(BF16), wf_ref[...])
    yd = _dot(dn_ref[...].astype(BF16), wd_ref[...])
    o_ref[...] = (jax.nn.sigmoid(ga_ref[...]) * yf + jax.nn.sigmoid(gb_ref[...]) * yd).astype(o_ref.dtype)


def _mix_call(fo, dn, proj, wf, wd, tm):
    t = fo.shape[0]
    d = wf.shape[1]
    tn = 1024
    nb = d // tn
    return pl.pallas_call(
        _mix_kernel,
        grid=(t // tm, nb),
        in_specs=[
            pl.BlockSpec((tm, FOX_W), lambda i, j: (i, 0)),
            pl.BlockSpec((tm, FOX_W), lambda i, j: (i, 0)),
            pl.BlockSpec((tm, tn), lambda i, j: (i, 3 * nb + j)),
            pl.BlockSpec((tm, tn), lambda i, j: (i, 4 * nb + j)),
            pl.BlockSpec((FOX_W, tn), lambda i, j: (0, j)),
            pl.BlockSpec((FOX_W, tn), lambda i, j: (0, j)),
        ],
        out_specs=pl.BlockSpec((tm, tn), lambda i, j: (i, j)),
        out_shape=SDS((t, d), BF16),
        compiler_params=_cp("parallel", "arbitrary"),
        name="mix",
    )(fo, dn, proj, proj, wf, wd)


def _resid_ln_kernel(x_ref, y_ref, gate_ref, w_ref, g_ref, b_ref, o_ref, *, alpha):
    mix = _dot(y_ref[...], w_ref[...])
    z = alpha * x_ref[...] + gate_ref[...] * mix
    o_ref[...] = _ln(z) * g_ref[...] + b_ref[...]


def _mod_spec(r, tm, d):
    return pl.BlockSpec((1, d), lambda i: (0, 0)) if r == 1 else pl.BlockSpec((tm, d), lambda i: (i, 0))


def _resid_ln_call(x, y, gate, w, g, b, tm, alpha):
    t, d = x.shape
    row = pl.BlockSpec((tm, d), lambda i: (i, 0))
    vec = pl.BlockSpec((1, d), lambda i: (0, 0))
    return pl.pallas_call(
        functools.partial(_resid_ln_kernel, alpha=alpha),
        grid=(t // tm,),
        in_specs=[row, row, _mod_spec(gate.shape[0], tm, d), pl.BlockSpec((d, d), lambda i: (0, 0)), vec, vec],
        out_specs=row,
        out_shape=SDS((t, d), F32),
        compiler_params=_cp("parallel"),
        name="resid_ln1",
    )(x, y, gate, w, g, b)


def _final_ln_kernel(x_ref, ft_ref, gate_ref, g_ref, b_ref, o_ref, *, alpha):
    z = alpha * x_ref[...] + gate_ref[...] * ft_ref[...].T
    o_ref[...] = _ln(z) * g_ref[...] + b_ref[...]


def _final_ln_call(x, ffn_t, gate, g, b, tm, alpha):
    t, d = x.shape
    row = pl.BlockSpec((tm, d), lambda i: (i, 0))
    vec = pl.BlockSpec((1, d), lambda i: (0, 0))
    return pl.pallas_call(
        functools.partial(_final_ln_kernel, alpha=alpha),
        grid=(t // tm,),
        in_specs=[row, pl.BlockSpec((d, tm), lambda i: (0, i)), _mod_spec(gate.shape[0], tm, d), vec, vec],
        out_specs=row,
        out_shape=SDS((t, d), F32),
        compiler_params=_cp("parallel"),
        name="final_ln2",
    )(x, ffn_t, gate, g, b)


def _peer_q_kernel(x_ref, sc_ref, sh_ref, wt_ref, ht_ref, qt_ref):
    h = _ln(x_ref[...]) * (1.0 + sc_ref[...]) + sh_ref[...]
    ht = h.T.astype(BF16)
    ht_ref[...] = ht
    qt_ref[...] = _dot(wt_ref[...], ht)


def _peer_q_call(x, sc, sh, wqt, tm):
    t, d = x.shape
    n = wqt.shape[0]
    mod = _mod_spec(sc.shape[0], tm, d)
    return pl.pallas_call(
        _peer_q_kernel,
        grid=(t // tm,),
        in_specs=[pl.BlockSpec((tm, d), lambda i: (i, 0)), mod, mod, pl.BlockSpec((n, d), lambda i: (0, 0))],
        out_specs=[pl.BlockSpec((d, tm), lambda i: (0, i)), pl.BlockSpec((n, tm), lambda i: (0, i))],
        out_shape=[SDS((d, t), BF16), SDS((n, t), F32)],
        compiler_params=_cp("parallel"),
        name="peer_query",
    )(x, sc, sh, wqt)


def _top_rows(x, k, with_rank=False):
    cur = x
    vals = []
    rank = jnp.full(x.shape, float(k), F32) if with_rank else None
    for a in range(k):
        m = jnp.max(cur, axis=0, keepdims=True)
        vals.append(m)
        hit = cur == m
        if with_rank:
            rank = jnp.where(hit, float(a), rank)
        cur = jnp.where(hit, NEG_INF, cur)
    return (vals, rank) if with_rank else vals


def _peer_sel_kernel(qt_ref, k1_ref, k2_ref, r2_ref, e2_ref, nsel_ref, e1_ref):
    s1 = _dot(k1_ref[...].astype(BF16), qt_ref[:N_KEYS, :].astype(BF16))
    s2 = _dot(k2_ref[...].astype(BF16), qt_ref[N_KEYS:, :].astype(BF16))
    v1 = _top_rows(s1, PEER_TOPK)
    v2, rank2 = _top_rows(s2, PEER_TOPK, with_rank=True)
    v2s = jnp.concatenate(v2, axis=0)
    cand = jnp.concatenate([v1[a] + v2s for a in range(PEER_TOPK)], axis=0)
    top = _top_rows(cand, PEER_TOPK)
    cmax, tau = top[0], top[-1]
    sel = cand >= tau
    z = jnp.sum(jnp.where(sel, jnp.exp(cand - cmax), 0.0), axis=0, keepdims=True)
    nsel = jnp.zeros(s1.shape, F32)
    for a in range(PEER_TOPK):
        grp = slice(a * PEER_TOPK, (a + 1) * PEER_TOPK)
        n_a = jnp.sum(jnp.where(cand[grp, :] >= tau, 1.0, 0.0), axis=0, keepdims=True)
        nsel = jnp.where(s1 == v1[a], n_a, nsel)
    r2_ref[...] = rank2.astype(BF16)
    e2_ref[...] = jnp.exp(s2 - v2[0]).astype(BF16)
    nsel_ref[...] = nsel
    e1_ref[...] = jnp.exp(s1 - v1[0]) / z


def _peer_sel_call(qt, k1, k2, tm):
    t = qt.shape[1]
    kd = 2 * N_KEYS
    key_spec = pl.BlockSpec((N_KEYS, N_KEYS), lambda i, h: (0, 0))
    st = pl.BlockSpec((None, N_KEYS, tm), lambda i, h: (h, 0, i))
    return pl.pallas_call(
        _peer_sel_kernel,
        grid=(t // tm, PEER_HEADS),
        in_specs=[pl.BlockSpec((kd, tm), lambda i, h: (h, i)), key_spec, key_spec],
        out_specs=[st, st, st, st],
        out_shape=[SDS((PEER_HEADS, N_KEYS, t), BF16), SDS((PEER_HEADS, N_KEYS, t), BF16),
                   SDS((PEER_HEADS, N_KEYS, t), F32), SDS((PEER_HEADS, N_KEYS, t), F32)],
        compiler_params=_cp("parallel", "arbitrary"),
        name="peer_select",
    )(qt, k1, k2)


GATE_ROWS = 16


def _gelu(x):
    return 0.5 * x * (1.0 + lax.erf(x * (1.0 / math.sqrt(2.0))))


def _build_gates(r2_ref, e2_ref, nsel_rows_ref, e1_rows_ref, row0, dst_ref, r_lo, r_hi):
    tm = dst_ref.shape[1]
    n_tiles = N_KEYS // GATE_ROWS
    zero = jnp.zeros((GATE_ROWS, tm), BF16)
    for r in range(r_lo, r_hi):
        acc = [zero] * n_tiles
        for h in range(PEER_HEADS):
            n_b = jnp.broadcast_to(nsel_rows_ref[h, row0 + r:row0 + r + 1, :], (GATE_ROWS, tm)).astype(BF16)
            e1_b = jnp.broadcast_to(e1_rows_ref[h, row0 + r:row0 + r + 1, :], (GATE_ROWS, tm)).astype(BF16)
            for t in range(n_tiles):
                ks = slice(t * GATE_ROWS, (t + 1) * GATE_ROWS)
                acc[t] = acc[t] + jnp.where(r2_ref[h, ks, :] < n_b, e1_b * e2_ref[h, ks, :], zero)
        for t in range(n_tiles):
            dst_ref[r * N_KEYS + t * GATE_ROWS:r * N_KEYS + (t + 1) * GATE_ROWS, :] = acc[t]


def _peer_dense_kernel(h2t_ref, u_ref, vt_ref, r2_ref, e2_ref, nsel_rows_ref, e1_rows_ref, o_ref, g0_scr, g1_scr,
                       *, te):
    j = pl.program_id(1)
    keys_per_step = te // N_KEYS
    n_chunks = te // MXU_DIM
    gate_refs = (r2_ref, e2_ref, nsel_rows_ref, e1_rows_ref)

    @pl.when(j == 0)
    def _():
        o_ref[...] = jnp.zeros_like(o_ref)
        _build_gates(*gate_refs, 0, g0_scr, 0, keys_per_step)

    def step(cur_scr, nxt_scr, nxt_row0):
        n_slices = 2 * n_chunks
        bounds = [keys_per_step * q // n_slices for q in range(n_slices + 1)]
        q = 0
        acts = []
        for c in range(n_chunks):
            rows = slice(c * MXU_DIM, (c + 1) * MXU_DIM)
            acts.append(_gelu(_dot(u_ref[rows, :], h2t_ref[...])).astype(BF16))
            _build_gates(*gate_refs, nxt_row0, nxt_scr, bounds[q], bounds[q + 1])
            q += 1
        upd = None
        for c in range(n_chunks):
            rows = slice(c * MXU_DIM, (c + 1) * MXU_DIM)
            g = cur_scr[rows, :] * acts[c]
            d = _dot(vt_ref[:, rows], g)
            upd = d if upd is None else upd + d
            _build_gates(*gate_refs, nxt_row0, nxt_scr, bounds[q], bounds[q + 1])
            q += 1
        o_ref[...] += upd

    @pl.when(lax.rem(j, 2) == 0)
    def _():
        step(g0_scr, g1_scr, keys_per_step)

    @pl.when(lax.rem(j, 2) == 1)
    def _():
        step(g1_scr, g0_scr, 0)


def _peer_dense_call(h2t, u_bf, vt_bf, r2, e2, nsel, e1, tm, te):
    d, t = h2t.shape
    ne = u_bf.shape[0]
    nj = ne // te
    keys_per_step = te // N_KEYS
    assert nj % 2 == 0 and 2 * keys_per_step == 8, "a sublane-aligned 8-row block of keys spans two expert tiles"
    st = pl.BlockSpec((PEER_HEADS, N_KEYS, tm), lambda i, j: (0, 0, i))
    nxt_rows = pl.BlockSpec((PEER_HEADS, 2 * keys_per_step, tm), lambda i, j: (0, lax.rem(j + 1, nj) // 2, i))
    return pl.pallas_call(
        functools.partial(_peer_dense_kernel, te=te),
        grid=(t // tm, nj),
        in_specs=[
            pl.BlockSpec((d, tm), lambda i, j: (0, i)),
            pl.BlockSpec((te, d), lambda i, j: (j, 0)),
            pl.BlockSpec((d, te), lambda i, j: (0, j)),
            st, st, nxt_rows, nxt_rows,
        ],
        out_specs=pl.BlockSpec((d, tm), lambda i, j: (0, i)),
        out_shape=SDS((d, t), F32),
        scratch_shapes=[pltpu.VMEM((te, tm), BF16), pltpu.VMEM((te, tm), BF16)],
        compiler_params=_cp("parallel", "arbitrary"),
        name="peer_dense",
    )(h2t, u_bf, vt_bf, r2, e2, nsel, e1)


def _rope_tables(pos):
    half = HEAD_DIM // 2
    inv = ROPE_THETA ** (-jnp.arange(half, dtype=F32) / half)
    ang = pos.astype(F32)[:, None] * inv[None, :]
    cos, sin = jnp.cos(ang), jnp.sin(ang)
    return jnp.concatenate([cos, cos], axis=1), jnp.concatenate([-sin, sin], axis=1)


def _layer_weights(l, w_in, b_f, w_o_fox, w_o_diff, w_out, peer_w_q, peer_u, peer_v):
    wl = w_in[l]
    c_fl = 3 * FOX_W
    w_r = jnp.concatenate(
        [wl[:, :c_fl], wl[:, c_fl + FOX_HEADS:], jnp.pad(wl[:, c_fl:c_fl + FOX_HEADS], ((0, 0), (0, LANES - FOX_HEADS)))],
        axis=1).astype(BF16)
    bf_pad = jnp.pad(b_f[l][None, :], ((0, 0), (0, LANES - FOX_HEADS)))
    return dict(
        w_r=w_r, bf_pad=bf_pad,
        wf=w_o_fox[l].astype(BF16), wd=w_o_diff[l].astype(BF16), wo=w_out[l].astype(BF16),
        pqt=peer_w_q[l].T.astype(BF16), u=peer_u[l].astype(BF16), vt=peer_v[l].T.astype(BF16),
    )


def _tail(x2d, mods, proj, fo, dn, lw, vecs, alpha, tm, peer_tm, peer_te):
    _, _, gate1, shift2, scale2, gate2 = mods
    t = x2d.shape[0]
    y = _mix_call(fo, dn, proj, lw["wf"], lw["wd"], tm)
    x1 = _resid_ln_call(x2d, y, gate1, lw["wo"], vecs["ln1_g"], vecs["ln1_b"], tm, alpha)
    tp = -(-t // peer_tm) * peer_tm
    pad = tp - t
    x1p = jnp.pad(x1, ((0, pad), (0, 0))) if pad else x1
    sc2, sh2 = scale2, shift2
    if pad and sc2.shape[0] != 1:
        sc2 = jnp.pad(sc2, ((0, pad), (0, 0)))
        sh2 = jnp.pad(sh2, ((0, pad), (0, 0)))
    h2t, qt = _peer_q_call(x1p, sc2, sh2, lw["pqt"], min(peer_tm, 256))
    r2, e2, nsel, e1 = _peer_sel_call(qt, vecs["pk1"], vecs["pk2"], peer_tm)
    ffn_t = _peer_dense_call(h2t, lw["u"], lw["vt"], r2, e2, nsel, e1, peer_tm, peer_te)
    if pad:
        ffn_t = ffn_t[:, :t]
    return _final_ln_call(x1, ffn_t, gate2, vecs["ln2_g"], vecs["ln2_b"], tm, alpha)


def kernel(x_prompt, x_sample, cache_fox_k, cache_fox_v, cache_fox_logf, cache_diff_k, cache_diff_v, page_table, c_prompt, c_sample, w_ada, b_ada, w_in, b_f, lambda_q1, lambda_k1, lambda_q2, lambda_k2, diff_subln_w, w_o_fox, w_o_diff, w_out, ln1_g, ln1_b, peer_w_q, peer_sub_k1, peer_sub_k2, peer_u, peer_v, ln2_g, ln2_b):
    depth = w_ada.shape[0]
    bp, sp, d = x_prompt.shape
    bs, ss, _ = x_sample.shape
    assert bp == 1 and ss == 1, "one prompt sequence and one new token per sample sequence"
    n_pages = page_table.shape[1]
    page = cache_fox_k.shape[2]
    n_pool = cache_fox_k.shape[1]
    n_past = n_pages * page
    alpha = (2 * depth) ** 0.25
    dv = DIFF_V
    pg = 16

    cos_p, sin_p = _rope_tables(jnp.arange(sp))
    cos_s, sin_s = _rope_tables(n_past + jnp.arange(ss))
    lf_flat = cache_fox_logf.reshape(depth, n_pool, 1, page * FOX_HEADS)

    c_all = jnp.concatenate([c_prompt, c_sample], axis=0)
    n_c = c_all.shape[0]
    c_all = jnp.pad(c_all, ((0, -n_c % 8), (0, 0)))

    xp = x_prompt.reshape(sp, d)
    xs = x_sample.reshape(bs, d)
    st_p, st_s = [], []
    for l in range(depth):
        lam_init = 0.8 - 0.6 * math.exp(-0.3 * l)
        lw = _layer_weights(l, w_in, b_f, w_o_fox, w_o_diff, w_out, peer_w_q, peer_u, peer_v)
        vecs = dict(ln1_g=ln1_g[l][None], ln1_b=ln1_b[l][None], ln2_g=ln2_g[l][None], ln2_b=ln2_b[l][None],
                    pk1=peer_sub_k1[l], pk2=peer_sub_k2[l])
        lam_vecs = [v[l][None] for v in (lambda_q1, lambda_k1, lambda_q2, lambda_k2)]
        subln = diff_subln_w[l][None]
        subln_rep = jnp.broadcast_to(diff_subln_w[l][:, None], (dv, LANES))

        mod = _ada_call(c_all, w_ada, b_ada, l)
        mods_p = [mod[0:bp, i * d:(i + 1) * d] for i in range(6)]
        mods_s = [mod[bp:bp + bs, i * d:(i + 1) * d] for i in range(6)]

        proj = _lnmod_mm_call(xp, mods_p[1], mods_p[0], lw["w_r"], 512, 1152)
        (fk_o, fv_o, dk_o, dv_o, logf, fqt, fkb, fvt, dqt, dkb, dvt, ckb, cumt) = _post_prompt_call(
            proj, cos_p, sin_p, lw["bf_pad"], 256)
        fo = _fox_flash_call(fqt, fkb, fvt, cumt, ckb, 512)
        dn = _diff_flash_call(dqt, dkb, dvt, lam_vecs, subln_rep, 512, lam_init)
        st_p.append((fk_o.reshape(bp, sp, FOX_HEADS, HEAD_DIM), fv_o.reshape(bp, sp, FOX_HEADS, HEAD_DIM),
                     logf.reshape(bp, sp, FOX_HEADS),
                     dk_o.reshape(bp, sp, DIFF_HEADS, dv), dv_o.reshape(bp, sp, DIFF_HEADS, dv)))
        xp = _tail(xp, mods_p, proj, fo, dn, lw, vecs, alpha, 256, 512, 512)

        proj_s = _lnmod_mm_call(xs, mods_s[1], mods_s[0], lw["w_r"], bs, 1152)
        dq_s, dk_s, logf_s = _post_call(proj_s, cos_s, sin_s, lw["bf_pad"], bs)
        fk_s = proj_s[:, FOX_W:2 * FOX_W].reshape(bs, FOX_HEADS, HEAD_DIM)
        fv_s = proj_s[:, 2 * FOX_W:3 * FOX_W].reshape(bs, FOX_HEADS, HEAD_DIM)
        dv_s = proj_s[:, 5 * FOX_W:6 * FOX_W].reshape(bs, DIFF_HEADS, dv)
        lfn_cls = jnp.tile(logf_s, (1, LANES // FOX_HEADS)).reshape(bs, 1, LANES)
        fo_s = _fox_dec_call(l, page_table, proj_s[:, :FOX_W].reshape(bs, FOX_HEADS, HEAD_DIM), fk_s, fv_s, lfn_cls,
                             cache_fox_k, cache_fox_v, lf_flat, pg)
        dn_s = _diff_dec_call(l, page_table, dq_s.reshape(bs, N_MAPS, HEAD_DIM), dk_s.reshape(bs, N_MAPS, HEAD_DIM),
                              jnp.repeat(dv_s, 2, axis=1), lam_vecs, subln, cache_diff_k, cache_diff_v, pg, lam_init)
        dn_s = dn_s[:, ::2, :].reshape(bs, FOX_W)
        st_s.append((fk_s.reshape(bs, ss, FOX_HEADS, HEAD_DIM), fv_s.reshape(bs, ss, FOX_HEADS, HEAD_DIM),
                     logf_s.reshape(bs, ss, FOX_HEADS), dk_s.reshape(bs, ss, DIFF_HEADS, dv),
                     dv_s.reshape(bs, ss, DIFF_HEADS, dv)))
        xs = _tail(xs, mods_s, proj_s, fo_s.reshape(bs, FOX_W), dn_s, lw, vecs, alpha, bs, LANES, 512)

    outs_p = [jnp.stack([s[i] for s in st_p]) for i in range(5)]
    outs_s = [jnp.stack([s[i] for s in st_s]) for i in range(5)]
    return (xp.reshape(bp, sp, d), xs.reshape(bs, ss, d), *outs_p, *outs_s)
```

```python
import functools
import math

import jax
import jax.numpy as jnp
import numpy as np
from jax import lax
from jax.experimental import pallas as pl
from jax.experimental.pallas import tpu as pltpu

F32 = jnp.float32
BF16 = jnp.bfloat16
NEG_INF = float("-inf")
POS_INF = float("inf")

LN_EPS = 1e-5
ROPE_THETA = 10000.0
HEAD_DIM = 128
FOX_HEADS = 8
DIFF_HEADS = 4
N_MAPS = 8
FOX_W = FOX_HEADS * HEAD_DIM
DIFF_V = 2 * HEAD_DIM
PEER_HEADS = 8
PEER_TOPK = 16
N_KEYS = 128
ATTN_SCALE = HEAD_DIM ** -0.5
LOG2E = 1.0 / math.log(2.0)
QK_SCALE_LOG2 = ATTN_SCALE * LOG2E
LANES = 128
MXU_DIM = 256
VMEM_LIMIT = 48 * 1024 * 1024

SDS = jax.ShapeDtypeStruct


def _cp(*sem):
    return pltpu.CompilerParams(dimension_semantics=sem, vmem_limit_bytes=VMEM_LIMIT)


def _ln(x):
    mu = jnp.mean(x, axis=-1, keepdims=True)
    xc = x - mu
    var = jnp.mean(xc * xc, axis=-1, keepdims=True)
    return xc * lax.rsqrt(var + LN_EPS)


def _dot_nt(a, b):
    return lax.dot_general(a, b, (((1,), (1,)), ((), ())), preferred_element_type=F32)


def _dot(a, b):
    return jnp.dot(a, b, preferred_element_type=F32)


def _rep_chunks(x, k):
    return x if k == 1 else jnp.concatenate([x] * k, axis=1)


def _ada_kernel(c_ref, w_ref, b_ref, o_ref):
    c = c_ref[...]
    a = (c * jax.nn.sigmoid(c)).astype(BF16)
    o_ref[...] = _dot(a, w_ref[...].astype(BF16)) + b_ref[...]


def _ada_call(c_all, w_ada, b_ada, layer):
    mp, d = c_all.shape
    n = w_ada.shape[2]
    tn = 1024
    return pl.pallas_call(
        _ada_kernel,
        grid=(n // tn,),
        in_specs=[
            pl.BlockSpec((mp, d), lambda j: (0, 0)),
            pl.BlockSpec((None, d, tn), lambda j: (layer, 0, j)),
            pl.BlockSpec((None, 1, tn), lambda j: (layer, 0, j)),
        ],
        out_specs=pl.BlockSpec((mp, tn), lambda j: (0, j)),
        out_shape=SDS((mp, n), F32),
        compiler_params=_cp("arbitrary"),
        name="ada",
    )(c_all, w_ada, b_ada.reshape(b_ada.shape[0], 1, n))


def _lnmod_mm_kernel(x_ref, sc_ref, sh_ref, w_ref, o_ref, h_scr):
    @pl.when(pl.program_id(1) == 0)
    def _():
        h = _ln(x_ref[...]) * (1.0 + sc_ref[...]) + sh_ref[...]
        h_scr[...] = h.astype(BF16)

    o_ref[...] = _dot(h_scr[...], w_ref[...])


def _mod_spec2(r, tm, d):
    return pl.BlockSpec((1, d), lambda i, j: (0, 0)) if r == 1 else pl.BlockSpec((tm, d), lambda i, j: (i, 0))


def _lnmod_mm_call(x, sc, sh, w, tm, tn):
    t, d = x.shape
    n = w.shape[1]
    mod_spec = _mod_spec2(sc.shape[0], tm, d)
    return pl.pallas_call(
        _lnmod_mm_kernel,
        grid=(t // tm, n // tn),
        in_specs=[
            pl.BlockSpec((tm, d), lambda i, j: (i, 0)),
            mod_spec,
            mod_spec,
            pl.BlockSpec((d, tn), lambda i, j: (0, j)),
        ],
        out_specs=pl.BlockSpec((tm, tn), lambda i, j: (i, j)),
        out_shape=SDS((t, n), F32),
        scratch_shapes=[pltpu.VMEM((tm, d), BF16)],
        compiler_params=_cp("parallel", "arbitrary"),
        name="lnmod_mm",
    )(x, sc, sh, w)


def _rope(x, c, s):
    return x * c + pltpu.roll(x, HEAD_DIM // 2, axis=1) * s


def _log_sigmoid(z):
    return jnp.minimum(z, 0.0) - jnp.log1p(jnp.exp(-jnp.abs(z)))


def _post_kernel(dq_ref, dk_ref, fl_ref, cos_ref, sin_ref, bf_ref, dqo_ref, dko_ref, lf_ref):
    c = cos_ref[...]
    s = sin_ref[...]
    for j in range(N_MAPS):
        sl = slice(j * HEAD_DIM, (j + 1) * HEAD_DIM)
        dqo_ref[:, sl] = _rope(dq_ref[:, sl], c, s)
        dko_ref[:, sl] = _rope(dk_ref[:, sl], c, s)
    lf_ref[...] = _log_sigmoid(fl_ref[...] + bf_ref[...])[:, :FOX_HEADS]


def _post_call(proj, cos2, sin2, bf_pad, tm):
    t = proj.shape[0]
    tab_spec = pl.BlockSpec((1, LANES), lambda i: (0, 0))
    row = pl.BlockSpec((tm, FOX_W), lambda i: (i, 0))
    return pl.pallas_call(
        _post_kernel,
        grid=(t // tm,),
        in_specs=[
            pl.BlockSpec((tm, FOX_W), lambda i: (i, 3)),
            pl.BlockSpec((tm, FOX_W), lambda i: (i, 4)),
            pl.BlockSpec((tm, LANES), lambda i: (i, 80)),
            tab_spec, tab_spec, tab_spec,
        ],
        out_specs=[row, row, pl.BlockSpec((tm, FOX_HEADS), lambda i: (i, 0))],
        out_shape=[SDS((t, FOX_W), F32), SDS((t, FOX_W), F32), SDS((t, FOX_HEADS), F32)],
        compiler_params=_cp("arbitrary"),
        name="post_proj",
    )(proj, proj, proj, cos2, sin2, bf_pad)


def _post_prompt_kernel(fq_ref, fk_ref, fv_ref, dq_ref, dk_ref, dv_ref, fl_ref, cos_ref, sin_ref, bf_ref,
                        fko_ref, fvo_ref, dko_ref, dvo_ref, lf_ref,
                        fqt_ref, fkb_ref, fvt_ref, dqt_ref, dkb_ref, dvt_ref, ckb_ref, cumt_ref,
                        carry_scr, *, tm):
    c = cos_ref[...]
    s = sin_ref[...]
    for j in range(N_MAPS):
        sl = slice(j * HEAD_DIM, (j + 1) * HEAD_DIM)
        half = slice((j % 2) * HEAD_DIM, (j % 2 + 1) * HEAD_DIM)
        dk = _rope(dk_ref[:, sl], c, s)
        dko_ref[:, j // 2, half] = dk
        dkb_ref[:, sl] = dk.astype(BF16)
        dqt_ref[sl, :] = (_rope(dq_ref[:, sl], c, s) * QK_SCALE_LOG2).T.astype(BF16)
        fqt_ref[sl, :] = (fq_ref[:, sl] * QK_SCALE_LOG2).T.astype(BF16)
        fk = fk_ref[:, sl]
        fko_ref[:, j, :] = fk
        fkb_ref[:, sl] = fk.astype(BF16)
        fv = fv_ref[:, sl]
        fvo_ref[:, j, :] = fv
        fvt_ref[sl, :] = fv.T.astype(BF16)
        dv = dv_ref[:, sl]
        dvo_ref[:, j // 2, half] = dv
        dvt_ref[sl, :] = dv.T.astype(BF16)
    lf = _log_sigmoid(fl_ref[...] + bf_ref[...])
    lf_ref[...] = lf[:, :FOX_HEADS]

    @pl.when(pl.program_id(0) == 0)
    def _():
        carry_scr[...] = jnp.zeros_like(carry_scr)

    row = lax.broadcasted_iota(jnp.int32, (tm, tm), 0)
    col = lax.broadcasted_iota(jnp.int32, (tm, tm), 1)
    tri = jnp.where(col <= row, 1.0, 0.0).astype(BF16)
    hi = lf.astype(BF16)
    r1 = lf - hi.astype(F32)
    mid = r1.astype(BF16)
    lo = (r1 - mid.astype(F32)).astype(BF16)
    cum = (_dot(tri, hi) + _dot(tri, mid) + _dot(tri, lo)) + carry_scr[...]
    carry_scr[...] = cum[tm - 1:tm, :]
    cum2 = cum * LOG2E
    cumt_ref[...] = cum2.T[:FOX_HEADS, :]
    for h in range(FOX_HEADS):
        ckb_ref[h] = jnp.broadcast_to(cum2[:, h:h + 1], (tm, LANES))


def _post_prompt_call(proj, cos2, sin2, bf_pad, tm):
    t = proj.shape[0]
    col = lambda c: pl.BlockSpec((tm, FOX_W), lambda i: (i, c))
    tab = pl.BlockSpec((tm, LANES), lambda i: (i, 0))
    row_f32 = pl.BlockSpec((tm, FOX_W), lambda i: (i, 0))
    tr = pl.BlockSpec((FOX_W, tm), lambda i: (0, i))
    fox_st = pl.BlockSpec((tm, FOX_HEADS, HEAD_DIM), lambda i: (i, 0, 0))
    diff_st = pl.BlockSpec((tm, DIFF_HEADS, DIFF_V), lambda i: (i, 0, 0))
    return pl.pallas_call(
        functools.partial(_post_prompt_kernel, tm=tm),
        grid=(t // tm,),
        in_specs=[col(0), col(1), col(2), col(3), col(4), col(5),
                  pl.BlockSpec((tm, LANES), lambda i: (i, 80)), tab, tab,
                  pl.BlockSpec((1, LANES), lambda i: (0, 0))],
        out_specs=[fox_st, fox_st, diff_st, diff_st, pl.BlockSpec((tm, FOX_HEADS), lambda i: (i, 0)),
                   tr, row_f32, tr, tr, row_f32, tr,
                   pl.BlockSpec((FOX_HEADS, tm, LANES), lambda i: (0, i, 0)),
                   pl.BlockSpec((FOX_HEADS, tm), lambda i: (0, i))],
        out_shape=[SDS((t, FOX_HEADS, HEAD_DIM), F32), SDS((t, FOX_HEADS, HEAD_DIM), F32),
                   SDS((t, DIFF_HEADS, DIFF_V), F32), SDS((t, DIFF_HEADS, DIFF_V), F32), SDS((t, FOX_HEADS), F32),
                   SDS((FOX_W, t), BF16), SDS((t, FOX_W), BF16), SDS((FOX_W, t), BF16),
                   SDS((FOX_W, t), BF16), SDS((t, FOX_W), BF16), SDS((FOX_W, t), BF16),
                   SDS((FOX_HEADS, t, LANES), F32), SDS((FOX_HEADS, t), F32)],
        scratch_shapes=[pltpu.VMEM((1, LANES), F32)],
        compiler_params=_cp("arbitrary"),
        name="post_prompt",
    )(proj, proj, proj, proj, proj, proj, proj, cos2, sin2, bf_pad)


def _tri_pairs(n):
    qi = np.concatenate([np.full(i + 1, i, np.int32) for i in range(n)])
    ki = np.concatenate([np.arange(i + 1, dtype=np.int32) for i in range(n)])
    return jnp.asarray(qi), jnp.asarray(ki)


def _flash_update(st, vt_bf, m_scr, l_scr, acc_scr, idx, rows):
    m_prev = m_scr[idx:idx + 1, :]
    m_new = jnp.maximum(m_prev, jnp.max(st, axis=0, keepdims=True))
    alpha = jnp.exp2(m_prev - m_new)
    p = jnp.exp2(st - m_new)
    l_scr[idx:idx + 1, :] = alpha * l_scr[idx:idx + 1, :] + jnp.sum(p, axis=0, keepdims=True)
    acc_scr[rows, :] = alpha * acc_scr[rows, :] + _dot(vt_bf, p.astype(BF16))
    m_scr[idx:idx + 1, :] = m_new


def _flash_init(m_scr, l_scr, acc_scr):
    m_scr[...] = jnp.full_like(m_scr, NEG_INF)
    l_scr[...] = jnp.zeros_like(l_scr)
    acc_scr[...] = jnp.zeros_like(acc_scr)


def _causal_keep_t(tk, tq):
    kpos = lax.broadcasted_iota(jnp.int32, (tk, tq), 0)
    qpos = lax.broadcasted_iota(jnp.int32, (tk, tq), 1)
    return kpos <= qpos


def _fox_flash_kernel(qi_tab, ki_tab, qt_ref, k_ref, vt_ref, cq_ref, ckb_ref, o_ref, m_scr, l_scr, acc_scr, *, tq, tk):
    pid = pl.program_id(0)
    qi = qi_tab[pid]
    ki = ki_tab[pid]

    @pl.when(ki == 0)
    def _():
        _flash_init(m_scr, l_scr, acc_scr)

    def body(masked):
        keep = _causal_keep_t(tk, tq) if masked else None
        for h in range(FOX_HEADS):
            rows = slice(h * HEAD_DIM, (h + 1) * HEAD_DIM)
            st = _dot(k_ref[:, rows], qt_ref[rows, :])
            st = st + (cq_ref[h:h + 1, :] - _rep_chunks(ckb_ref[h], tq // LANES))
            if masked:
                st = jnp.where(keep, st, NEG_INF)
            _flash_update(st, vt_ref[rows, :], m_scr, l_scr, acc_scr, h, rows)

    @pl.when(ki < qi)
    def _():
        body(False)

    @pl.when(ki == qi)
    def _():
        body(True)
        for h in range(FOX_HEADS):
            rows = slice(h * HEAD_DIM, (h + 1) * HEAD_DIM)
            o_ref[:, rows] = (acc_scr[rows, :] / l_scr[h:h + 1, :]).T.astype(o_ref.dtype)


def _fox_flash_call(fqt, fkb, fvt, cumt, ckb, tq):
    t = fkb.shape[0]
    nq = t // tq
    qi_tab, ki_tab = _tri_pairs(nq)
    grid_spec = pltpu.PrefetchScalarGridSpec(
        num_scalar_prefetch=2,
        grid=(int(qi_tab.shape[0]),),
        in_specs=[
            pl.BlockSpec((FOX_W, tq), lambda p, qt, kt: (0, qt[p])),
            pl.BlockSpec((tq, FOX_W), lambda p, qt, kt: (kt[p], 0)),
            pl.BlockSpec((FOX_W, tq), lambda p, qt, kt: (0, kt[p])),
            pl.BlockSpec((FOX_HEADS, tq), lambda p, qt, kt: (0, qt[p])),
            pl.BlockSpec((FOX_HEADS, tq, LANES), lambda p, qt, kt: (0, kt[p], 0)),
        ],
        out_specs=pl.BlockSpec((tq, FOX_W), lambda p, qt, kt: (qt[p], 0)),
        scratch_shapes=[
            pltpu.VMEM((N_MAPS, tq), F32),
            pltpu.VMEM((N_MAPS, tq), F32),
            pltpu.VMEM((FOX_W, tq), F32),
        ],
    )
    return pl.pallas_call(
        functools.partial(_fox_flash_kernel, tq=tq, tk=tq),
        grid_spec=grid_spec,
        out_shape=SDS((t, FOX_W), BF16),
        compiler_params=_cp("arbitrary"),
        name="fox_flash",
    )(qi_tab, ki_tab, fqt, fkb, fvt, cumt, ckb)


def _lambda_full(lq1_ref, lk1_ref, lq2_ref, lk2_ref, lam_init):
    a = jnp.exp(jnp.sum(lq1_ref[...] * lk1_ref[...], axis=1, keepdims=True))
    b = jnp.exp(jnp.sum(lq2_ref[...] * lk2_ref[...], axis=1, keepdims=True))
    return a - b + lam_init


def _diff_flash_kernel(qi_tab, ki_tab, qt_ref, k_ref, vt_ref, lq1_ref, lk1_ref, lq2_ref, lk2_ref, sub_ref,
                       o_ref, m_scr, l_scr, acc_scr, *, tq, tk, lam_init):
    pid = pl.program_id(0)
    qi = qi_tab[pid]
    ki = ki_tab[pid]

    @pl.when(ki == 0)
    def _():
        _flash_init(m_scr, l_scr, acc_scr)

    def body(masked):
        keep = _causal_keep_t(tk, tq) if masked else None
        for mp in range(N_MAPS):
            hd = mp // 2
            rows = slice(mp * HEAD_DIM, (mp + 1) * HEAD_DIM)
            st = _dot(k_ref[:, rows], qt_ref[rows, :])
            if masked:
                st = jnp.where(keep, st, NEG_INF)
            _flash_update(st, vt_ref[hd * DIFF_V:(hd + 1) * DIFF_V, :], m_scr, l_scr, acc_scr, mp,
                          slice(mp * DIFF_V, (mp + 1) * DIFF_V))

    @pl.when(ki < qi)
    def _():
        body(False)

    @pl.when(ki == qi)
    def _():
        body(True)
        lam = _lambda_full(lq1_ref, lk1_ref, lq2_ref, lk2_ref, lam_init)
        sub = _rep_chunks(sub_ref[...], tq // LANES)
        for hd in range(DIFF_HEADS):
            o1 = acc_scr[(2 * hd) * DIFF_V:(2 * hd + 1) * DIFF_V, :] / l_scr[2 * hd:2 * hd + 1, :]
            o2 = acc_scr[(2 * hd + 1) * DIFF_V:(2 * hd + 2) * DIFF_V, :] / l_scr[2 * hd + 1:2 * hd + 2, :]
            d = o1 - lam * o2
            y = d * lax.rsqrt(jnp.mean(d * d, axis=0, keepdims=True) + LN_EPS)
            o_ref[:, hd * DIFF_V:(hd + 1) * DIFF_V] = ((y * sub) * (1.0 - lam_init)).T.astype(o_ref.dtype)


def _diff_flash_call(dqt, dkb, dvt, lam_vecs, subln_rep, tq, lam_init):
    t = dkb.shape[0]
    nq = t // tq
    qi_tab, ki_tab = _tri_pairs(nq)
    vec = pl.BlockSpec((1, HEAD_DIM), lambda p, qt, kt: (0, 0))
    grid_spec = pltpu.PrefetchScalarGridSpec(
        num_scalar_prefetch=2,
        grid=(int(qi_tab.shape[0]),),
        in_specs=[
            pl.BlockSpec((FOX_W, tq), lambda p, qt, kt: (0, qt[p])),
            pl.BlockSpec((tq, FOX_W), lambda p, qt, kt: (kt[p], 0)),
            pl.BlockSpec((FOX_W, tq), lambda p, qt, kt: (0, kt[p])),
            vec, vec, vec, vec,
            pl.BlockSpec((DIFF_V, LANES), lambda p, qt, kt: (0, 0)),
        ],
        out_specs=pl.BlockSpec((tq, FOX_W), lambda p, qt, kt: (qt[p], 0)),
        scratch_shapes=[
            pltpu.VMEM((N_MAPS, tq), F32),
            pltpu.VMEM((N_MAPS, tq), F32),
            pltpu.VMEM((N_MAPS * DIFF_V, tq), F32),
        ],
    )
    return pl.pallas_call(
        functools.partial(_diff_flash_kernel, tq=tq, tk=tq, lam_init=lam_init),
        grid_spec=grid_spec,
        out_shape=SDS((t, FOX_W), BF16),
        compiler_params=_cp("arbitrary"),
        name="diff_flash",
    )(qi_tab, ki_tab, dqt, dkb, dvt, *lam_vecs, subln_rep)


def _class_allreduce(x, op, period):
    s = period
    while s < LANES:
        x = op(x, pltpu.roll(x, s, axis=1))
        s *= 2
    return x


def _fold_chunks(x, op):
    out = x[:, :LANES]
    for c in range(1, x.shape[1] // LANES):
        out = op(out, x[:, c * LANES:(c + 1) * LANES])
    return out


def _suffix_incl(x, period):
    n = x.shape[1]
    lane = lax.broadcasted_iota(jnp.int32, x.shape, 1)
    s = period
    while s < n:
        y = pltpu.roll(x, n - s, axis=1)
        x = x + jnp.where(lane < n - s, y, 0.0)
        s *= 2
    return x


def _fox_dec_kernel(pt_ref, q_ref, kn_ref, vn_ref, lfn_ref, *refs, pg, nj):
    del pt_ref
    k_refs, v_refs, lf_refs = refs[:pg], refs[pg:2 * pg], refs[2 * pg:3 * pg]
    o_ref = refs[3 * pg]
    m_scr, l_scr, acc_scr, carry_scr = refs[3 * pg + 1:]
    j = pl.program_id(1)
    nh = FOX_HEADS
    w = PAGE_ROWS_FOX
    sub = lax.broadcasted_iota(jnp.int32, (nh, w), 0)
    lane = lax.broadcasted_iota(jnp.int32, (nh, w), 1)
    diag = sub == (lane & (nh - 1))
    diag1 = diag[:, :LANES]

    def to_sub(x_cls):
        return jnp.max(jnp.where(diag1, jnp.broadcast_to(x_cls, (nh, LANES)), NEG_INF), axis=1, keepdims=True)

    @pl.when(j == 0)
    def _():
        m_scr[...] = jnp.full_like(m_scr, NEG_INF)
        l_scr[...] = jnp.zeros_like(l_scr)
        acc_scr[...] = jnp.zeros_like(acc_scr)
        carry_scr[...] = lfn_ref[...]

    qb = (q_ref[...] * ATTN_SCALE).astype(BF16)

    lf = jnp.concatenate([r[...] for r in lf_refs], axis=0)
    tot = _class_allreduce(_fold_chunks(lf, jnp.add), jnp.add, nh)
    run = carry_scr[...]
    before = []
    for i in range(pg):
        before.append(run)
        run = run + tot[i:i + 1, :]
    carry_scr[...] = run
    bias = (_suffix_incl(lf, nh) - lf) + _rep_chunks(jnp.concatenate(before, axis=0), w // LANES)

    rows = []
    for i in range(pg):
        k2 = k_refs[i][...].reshape(w, HEAD_DIM).astype(BF16)
        st = _dot_nt(qb, k2)
        rows.append(jnp.sum(jnp.where(diag, st, 0.0), axis=0, keepdims=True))
    s = jnp.concatenate(rows, axis=0) + bias

    mc = _class_allreduce(_fold_chunks(jnp.max(s, axis=0, keepdims=True), jnp.maximum), jnp.maximum, nh)
    m_old = m_scr[...]
    m_new = jnp.maximum(m_old, mc)
    alpha = jnp.exp(m_old - m_new)
    p = jnp.exp(s - _rep_chunks(m_new, w // LANES))
    l_scr[...] = l_scr[...] * _rep_chunks(alpha, w // LANES) + p

    acc_c = jnp.zeros((nh, HEAD_DIM), F32)
    for i in range(pg):
        pm = jnp.where(diag, jnp.broadcast_to(p[i:i + 1, :], (nh, w)), 0.0).astype(BF16)
        v2 = v_refs[i][...].reshape(w, HEAD_DIM).astype(BF16)
        acc_c = acc_c + _dot(pm, v2)
    acc_scr[...] = acc_scr[...] * to_sub(alpha) + acc_c
    m_scr[...] = m_new

    @pl.when(j == nj - 1)
    def _():
        kn = kn_ref[...].astype(BF16).astype(F32)
        s_self = jnp.sum(qb.astype(F32) * kn, axis=1, keepdims=True)
        s_cls = jnp.sum(jnp.where(diag1, jnp.broadcast_to(s_self, (nh, LANES)), 0.0), axis=0, keepdims=True)
        m_fin = jnp.maximum(m_new, s_cls)
        a2 = jnp.exp(m_new - m_fin)
        p_self = jnp.exp(s_cls - m_fin)
        l_cls = _class_allreduce(_fold_chunks(jnp.sum(l_scr[...], axis=0, keepdims=True), jnp.add), jnp.add, nh)
        l_cls = l_cls * a2 + p_self
        o = (acc_scr[...] * to_sub(a2) + to_sub(p_self) * vn_ref[...]) / to_sub(l_cls)
        o_ref[...] = o


PAGE_ROWS_FOX = 128 * FOX_HEADS
PAGE_ROWS_DIFF = 128 * DIFF_HEADS


def _fox_dec_call(layer, page_table, q, k_new, v_new, lfn_cls, cache_k, cache_v, cache_lf_flat, pg):
    b = q.shape[0]
    n_pages = page_table.shape[1]
    page = cache_k.shape[2]
    nj = n_pages // pg

    def page_idx(i):
        return lambda bb, j, pt: (layer, pt[bb, n_pages - 1 - (j * pg + i)], 0, 0, 0)

    def lf_idx(i):
        return lambda bb, j, pt: (layer, pt[bb, n_pages - 1 - (j * pg + i)], 0, 0)

    tok = pl.BlockSpec((None, FOX_HEADS, HEAD_DIM), lambda bb, j, pt: (bb, 0, 0))
    in_specs = [tok, tok, tok, pl.BlockSpec((None, 1, LANES), lambda bb, j, pt: (bb, 0, 0))]
    in_specs += [pl.BlockSpec((None, None, page, FOX_HEADS, HEAD_DIM), page_idx(i)) for i in range(pg)]
    in_specs += [pl.BlockSpec((None, None, page, FOX_HEADS, HEAD_DIM), page_idx(i)) for i in range(pg)]
    in_specs += [pl.BlockSpec((None, None, 1, page * FOX_HEADS), lf_idx(i)) for i in range(pg)]
    grid_spec = pltpu.PrefetchScalarGridSpec(
        num_scalar_prefetch=1,
        grid=(b, nj),
        in_specs=in_specs,
        out_specs=pl.BlockSpec((None, FOX_HEADS, HEAD_DIM), lambda bb, j, pt: (bb, 0, 0)),
        scratch_shapes=[
            pltpu.VMEM((1, LANES), F32),
            pltpu.VMEM((pg, page * FOX_HEADS), F32),
            pltpu.VMEM((FOX_HEADS, HEAD_DIM), F32),
            pltpu.VMEM((1, LANES), F32),
        ],
    )
    return pl.pallas_call(
        functools.partial(_fox_dec_kernel, pg=pg, nj=nj),
        grid_spec=grid_spec,
        out_shape=SDS((b, FOX_HEADS, HEAD_DIM), F32),
        compiler_params=_cp("parallel", "arbitrary"),
        name="fox_decode",
    )(page_table, q, k_new, v_new, lfn_cls, *([cache_k] * pg), *([cache_v] * pg), *([cache_lf_flat] * pg))


def _diff_dec_kernel(pt_ref, q_ref, kn_ref, vn_ref, lq1_ref, lk1_ref, lq2_ref, lk2_ref, sub_ref, *refs,
                     pg, nj, lam_init):
    del pt_ref
    k_refs, v_refs = refs[:pg], refs[pg:2 * pg]
    o_ref = refs[2 * pg]
    m_scr, l_scr, acc_scr = refs[2 * pg + 1:]
    j = pl.program_id(1)
    nh = DIFF_HEADS
    w = PAGE_ROWS_DIFF
    dv = DIFF_V
    sub = lax.broadcasted_iota(jnp.int32, (N_MAPS, w), 0)
    lane = lax.broadcasted_iota(jnp.int32, (N_MAPS, w), 1)
    same_head = (sub >> 1) == (lane & (nh - 1))
    sel = [jnp.logical_and(same_head, (sub & 1) == n) for n in range(2)]
    sel1 = [x[:, :LANES] for x in sel]

    def to_sub(x0, x1):
        y = jnp.where(sel1[0], jnp.broadcast_to(x0, (N_MAPS, LANES)),
                      jnp.where(sel1[1], jnp.broadcast_to(x1, (N_MAPS, LANES)), NEG_INF))
        return jnp.max(y, axis=1, keepdims=True)

    @pl.when(j == 0)
    def _():
        m_scr[...] = jnp.full_like(m_scr, NEG_INF)
        l_scr[...] = jnp.zeros_like(l_scr)
        acc_scr[...] = jnp.zeros_like(acc_scr)

    q8 = q_ref[...] * ATTN_SCALE
    odd = (lax.broadcasted_iota(jnp.int32, (N_MAPS, HEAD_DIM), 0) & 1) == 1
    qd = jnp.concatenate([jnp.where(odd, 0.0, q8), jnp.where(odd, q8, 0.0)], axis=1).astype(BF16)

    rows = [[], []]
    for i in range(pg):
        k2 = k_refs[i][...].reshape(w, dv).astype(BF16)
        st = _dot_nt(qd, k2)
        for n in range(2):
            rows[n].append(jnp.sum(jnp.where(sel[n], st, 0.0), axis=0, keepdims=True))
    s = [jnp.concatenate(rows[n], axis=0) for n in range(2)]

    p, alpha = [], []
    for n in range(2):
        mc = _class_allreduce(_fold_chunks(jnp.max(s[n], axis=0, keepdims=True), jnp.maximum), jnp.maximum, nh)
        m_old = m_scr[n]
        m_new = jnp.maximum(m_old, mc)
        a = jnp.exp(m_old - m_new)
        pn = jnp.exp(s[n] - _rep_chunks(m_new, w // LANES))
        l_scr[n] = l_scr[n] * _rep_chunks(a, w // LANES) + pn
        m_scr[n] = m_new
        p.append(pn)
        alpha.append(a)

    acc_c = jnp.zeros((N_MAPS, dv), F32)
    for i in range(pg):
        pm = jnp.where(sel[0], jnp.broadcast_to(p[0][i:i + 1, :], (N_MAPS, w)),
                       jnp.where(sel[1], jnp.broadcast_to(p[1][i:i + 1, :], (N_MAPS, w)), 0.0)).astype(BF16)
        v2 = v_refs[i][...].reshape(w, dv).astype(BF16)
        acc_c = acc_c + _dot(pm, v2)
    acc_scr[...] = acc_scr[...] * to_sub(alpha[0], alpha[1]) + acc_c

    @pl.when(j == nj - 1)
    def _():
        kn = kn_ref[...].astype(BF16).astype(F32)
        s_self = jnp.sum(q8.astype(BF16).astype(F32) * kn, axis=1, keepdims=True)
        a2, p_self, l_cls = [], [], []
        for n in range(2):
            s_cls = jnp.sum(jnp.where(sel1[n], jnp.broadcast_to(s_self, (N_MAPS, LANES)), 0.0),
                            axis=0, keepdims=True)
            m_new = m_scr[n]
            m_fin = jnp.maximum(m_new, s_cls)
            a = jnp.exp(m_new - m_fin)
            ps = jnp.exp(s_cls - m_fin)
            lc = _class_allreduce(_fold_chunks(jnp.sum(l_scr[n], axis=0, keepdims=True), jnp.add), jnp.add, nh)
            a2.append(a)
            p_self.append(ps)
            l_cls.append(lc * a + ps)
        o = (acc_scr[...] * to_sub(a2[0], a2[1]) + to_sub(p_self[0], p_self[1]) * vn_ref[...]) / to_sub(l_cls[0], l_cls[1])
        lam = _lambda_full(lq1_ref, lk1_ref, lq2_ref, lk2_ref, lam_init)
        d = o - lam * pltpu.roll(o, N_MAPS - 1, axis=0)
        y = d * lax.rsqrt(jnp.mean(d * d, axis=1, keepdims=True) + LN_EPS)
        o_ref[...] = (y * sub_ref[...]) * (1.0 - lam_init)


def _diff_dec_call(layer, page_table, q8, k_new8, v_new8, lam_vecs, subln, cache_k, cache_v, pg, lam_init):
    b = q8.shape[0]
    n_pages = page_table.shape[1]
    page = cache_k.shape[2]
    nj = n_pages // pg
    dv = DIFF_V

    def page_idx(i):
        return lambda bb, j, pt: (layer, pt[bb, n_pages - 1 - (j * pg + i)], 0, 0, 0)

    vec = pl.BlockSpec((1, HEAD_DIM), lambda bb, j, pt: (0, 0))
    tok = pl.BlockSpec((None, N_MAPS, HEAD_DIM), lambda bb, j, pt: (bb, 0, 0))
    in_specs = [tok, tok, pl.BlockSpec((None, N_MAPS, dv), lambda bb, j, pt: (bb, 0, 0)),
                vec, vec, vec, vec, pl.BlockSpec((1, dv), lambda bb, j, pt: (0, 0))]
    in_specs += [pl.BlockSpec((None, None, page, DIFF_HEADS, dv), page_idx(i)) for i in range(pg)]
    in_specs += [pl.BlockSpec((None, None, page, DIFF_HEADS, dv), page_idx(i)) for i in range(pg)]
    grid_spec = pltpu.PrefetchScalarGridSpec(
        num_scalar_prefetch=1,
        grid=(b, nj),
        in_specs=in_specs,
        out_specs=pl.BlockSpec((None, N_MAPS, dv), lambda bb, j, pt: (bb, 0, 0)),
        scratch_shapes=[
            pltpu.VMEM((2, 1, LANES), F32),
            pltpu.VMEM((2, pg, page * DIFF_HEADS), F32),
            pltpu.VMEM((N_MAPS, dv), F32),
        ],
    )
    return pl.pallas_call(
        functools.partial(_diff_dec_kernel, pg=pg, nj=nj, lam_init=lam_init),
        grid_spec=grid_spec,
        out_shape=SDS((b, N_MAPS, dv), F32),
        compiler_params=_cp("parallel", "arbitrary"),
        name="diff_decode",
    )(page_table, q8, k_new8, v_new8, *lam_vecs, subln, *([cache_k] * pg), *([cache_v] * pg))


def _mix_kernel(fo_ref, dn_ref, ga_ref, gb_ref, wf_ref, wd_ref, o_ref):
    yf = _dot(fo_ref[...].astype(BF16), wf_ref[...])
    yd = _dot(dn_ref[...].astype(BF16), wd_ref[...])
    o_ref[...] = (jax.nn.sigmoid(ga_ref[...]) * yf + jax.nn.sigmoid(gb_ref[...]) * yd).astype(o_ref.dtype)


def _mix_call(fo, dn, proj, wf, wd, tm):
    t = fo.shape[0]
    d = wf.shape[1]
    tn = 1024
    nb = d // tn
    return pl.pallas_call(
        _mix_kernel,
        grid=(t // tm, nb),
        in_specs=[
            pl.BlockSpec((tm, FOX_W), lambda i, j: (i, 0)),
            pl.BlockSpec((tm, FOX_W), lambda i, j: (i, 0)),
            pl.BlockSpec((tm, tn), lambda i, j: (i, 3 * nb + j)),
            pl.BlockSpec((tm, tn), lambda i, j: (i, 4 * nb + j)),
            pl.BlockSpec((FOX_W, tn), lambda i, j: (0, j)),
            pl.BlockSpec((FOX_W, tn), lambda i, j: (0, j)),
        ],
        out_specs=pl.BlockSpec((tm, tn), lambda i, j: (i, j)),
        out_shape=SDS((t, d), BF16),
        compiler_params=_cp("parallel", "arbitrary"),
        name="mix",
    )(fo, dn, proj, proj, wf, wd)


def _resid_ln_kernel(x_ref, y_ref, gate_ref, w_ref, g_ref, b_ref, o_ref, *, alpha):
    mix = _dot(y_ref[...], w_ref[...])
    z = alpha * x_ref[...] + gate_ref[...] * mix
    o_ref[...] = _ln(z) * g_ref[...] + b_ref[...]


def _mod_spec(r, tm, d):
    return pl.BlockSpec((1, d), lambda i: (0, 0)) if r == 1 else pl.BlockSpec((tm, d), lambda i: (i, 0))


def _resid_ln_call(x, y, gate, w, g, b, tm, alpha):
    t, d = x.shape
    row = pl.BlockSpec((tm, d), lambda i: (i, 0))
    vec = pl.BlockSpec((1, d), lambda i: (0, 0))
    return pl.pallas_call(
        functools.partial(_resid_ln_kernel, alpha=alpha),
        grid=(t // tm,),
        in_specs=[row, row, _mod_spec(gate.shape[0], tm, d), pl.BlockSpec((d, d), lambda i: (0, 0)), vec, vec],
        out_specs=row,
        out_shape=SDS((t, d), F32),
        compiler_params=_cp("parallel"),
        name="resid_ln1",
    )(x, y, gate, w, g, b)


def _final_ln_kernel(x_ref, ft_ref, gate_ref, g_ref, b_ref, o_ref, *, alpha):
    z = alpha * x_ref[...] + gate_ref[...] * ft_ref[...].T
    o_ref[...] = _ln(z) * g_ref[...] + b_ref[...]


def _final_ln_call(x, ffn_t, gate, g, b, tm, alpha):
    t, d = x.shape
    row = pl.BlockSpec((tm, d), lambda i: (i, 0))
    vec = pl.BlockSpec((1, d), lambda i: (0, 0))
    return pl.pallas_call(
        functools.partial(_final_ln_kernel, alpha=alpha),
        grid=(t // tm,),
        in_specs=[row, pl.BlockSpec((d, tm), lambda i: (0, i)), _mod_spec(gate.shape[0], tm, d), vec, vec],
        out_specs=row,
        out_shape=SDS((t, d), F32),
        compiler_params=_cp("parallel"),
        name="final_ln2",
    )(x, ffn_t, gate, g, b)


def _peer_q_kernel(x_ref, sc_ref, sh_ref, wt_ref, ht_ref, qt_ref):
    h = _ln(x_ref[...]) * (1.0 + sc_ref[...]) + sh_ref[...]
    ht = h.T.astype(BF16)
    ht_ref[...] = ht
    qt_ref[...] = _dot(wt_ref[...], ht)


def _peer_q_call(x, sc, sh, wqt, tm):
    t, d = x.shape
    n = wqt.shape[0]
    mod = _mod_spec(sc.shape[0], tm, d)
    return pl.pallas_call(
        _peer_q_kernel,
        grid=(t // tm,),
        in_specs=[pl.BlockSpec((tm, d), lambda i: (i, 0)), mod, mod, pl.BlockSpec((n, d), lambda i: (0, 0))],
        out_specs=[pl.BlockSpec((d, tm), lambda i: (0, i)), pl.BlockSpec((n, tm), lambda i: (0, i))],
        out_shape=[SDS((d, t), BF16), SDS((n, t), F32)],
        compiler_params=_cp("parallel"),
        name="peer_query",
    )(x, sc, sh, wqt)


def _top_rows(x, k, with_rank=False):
    cur = x
    vals = []
    rank = jnp.full(x.shape, float(k), F32) if with_rank else None
    for a in range(k):
        m = jnp.max(cur, axis=0, keepdims=True)
        vals.append(m)
        hit = cur == m
        if with_rank:
            rank = jnp.where(hit, float(a), rank)
        cur = jnp.where(hit, NEG_INF, cur)
    return (vals, rank) if with_rank else vals


def _peer_sel_kernel(qt_ref, k1_ref, k2_ref, r2_ref, e2_ref, nsel_ref, e1_ref):
    s1 = _dot(k1_ref[...].astype(BF16), qt_ref[:N_KEYS, :].astype(BF16))
    s2 = _dot(k2_ref[...].astype(BF16), qt_ref[N_KEYS:, :].astype(BF16))
    v1 = _top_rows(s1, PEER_TOPK)
    v2, rank2 = _top_rows(s2, PEER_TOPK, with_rank=True)
    v2s = jnp.concatenate(v2, axis=0)
    width = [PEER_TOPK // (a + 1) for a in range(PEER_TOPK)]
    blocks = [v1[a] + v2s[:width[a], :] for a in range(PEER_TOPK)]
    n_rows = sum(width)
    pad = -n_rows % 8
    cand = jnp.concatenate(blocks + [jnp.full((pad, s1.shape[1]), NEG_INF, F32)], axis=0)
    top = _top_rows(cand, PEER_TOPK)
    cmax, tau = top[0], top[-1]
    z = jnp.sum(jnp.where(cand >= tau, jnp.exp(cand - cmax), 0.0), axis=0, keepdims=True)
    nsel = jnp.zeros(s1.shape, F32)
    for a in range(PEER_TOPK):
        n_a = jnp.sum(jnp.where(blocks[a] >= tau, 1.0, 0.0), axis=0, keepdims=True)
        nsel = jnp.where(s1 == v1[a], n_a, nsel)
    r2_ref[...] = rank2.astype(BF16)
    e2_ref[...] = jnp.exp(s2 - v2[0]).astype(BF16)
    nsel_ref[...] = nsel
    e1_ref[...] = jnp.exp(s1 - v1[0]) / z


def _peer_sel_call(qt, k1, k2, tm):
    t = qt.shape[1]
    kd = 2 * N_KEYS
    key_spec = pl.BlockSpec((N_KEYS, N_KEYS), lambda i, h: (0, 0))
    st = pl.BlockSpec((None, N_KEYS, tm), lambda i, h: (h, 0, i))
    return pl.pallas_call(
        _peer_sel_kernel,
        grid=(t // tm, PEER_HEADS),
        in_specs=[pl.BlockSpec((kd, tm), lambda i, h: (h, i)), key_spec, key_spec],
        out_specs=[st, st, st, st],
        out_shape=[SDS((PEER_HEADS, N_KEYS, t), BF16), SDS((PEER_HEADS, N_KEYS, t), BF16),
                   SDS((PEER_HEADS, N_KEYS, t), F32), SDS((PEER_HEADS, N_KEYS, t), F32)],
        compiler_params=_cp("parallel", "arbitrary"),
        name="peer_select",
    )(qt, k1, k2)


GATE_ROWS = 16
ROW_BLOCK = 8


def _gelu(x):
    return 0.5 * x * (1.0 + lax.erf(x * (1.0 / math.sqrt(2.0))))


def _build_gates(r2_ref, e2_ref, nsel_rows_ref, e1_rows_ref, row0, dst_ref, r_lo, r_hi):
    tm = dst_ref.shape[1]
    n_tiles = N_KEYS // GATE_ROWS
    zero = jnp.zeros((GATE_ROWS, tm), BF16)
    for r in range(r_lo, r_hi):
        acc = [zero] * n_tiles
        for h in range(PEER_HEADS):
            n_b = jnp.broadcast_to(nsel_rows_ref[h, row0 + r:row0 + r + 1, :], (GATE_ROWS, tm)).astype(BF16)
            e1_b = jnp.broadcast_to(e1_rows_ref[h, row0 + r:row0 + r + 1, :], (GATE_ROWS, tm)).astype(BF16)
            for t in range(n_tiles):
                ks = slice(t * GATE_ROWS, (t + 1) * GATE_ROWS)
                acc[t] = acc[t] + jnp.where(r2_ref[h, ks, :] < n_b, e1_b * e2_ref[h, ks, :], zero)
        for t in range(n_tiles):
            dst_ref[r * N_KEYS + t * GATE_ROWS:r * N_KEYS + (t + 1) * GATE_ROWS, :] = acc[t]


def _peer_dense_kernel(h2t_ref, u_ref, vt_ref, r2_ref, e2_ref, nsel_rows_ref, e1_rows_ref, nsel_cur_ref, e1_cur_ref,
                       o_ref, g0_scr, g1_scr, *, te):
    j = pl.program_id(1)
    keys_per_step = te // N_KEYS
    n_chunks = te // MXU_DIM
    gate_refs = (r2_ref, e2_ref, nsel_rows_ref, e1_rows_ref)

    @pl.when(j == 0)
    def _():
        o_ref[...] = jnp.zeros_like(o_ref)
        _build_gates(r2_ref, e2_ref, nsel_cur_ref, e1_cur_ref, 0, g0_scr, 0, keys_per_step)

    def step(cur_scr, nxt_scr, nxt_row0):
        n_slices = 2 * n_chunks
        bounds = [keys_per_step * q // n_slices for q in range(n_slices + 1)]
        q = 0
        acts = []
        for c in range(n_chunks):
            rows = slice(c * MXU_DIM, (c + 1) * MXU_DIM)
            acts.append(_gelu(_dot(u_ref[rows, :], h2t_ref[...])).astype(BF16))
            _build_gates(*gate_refs, nxt_row0, nxt_scr, bounds[q], bounds[q + 1])
            q += 1
        upd = None
        for c in range(n_chunks):
            rows = slice(c * MXU_DIM, (c + 1) * MXU_DIM)
            g = cur_scr[rows, :] * acts[c]
            d = _dot(vt_ref[:, rows], g)
            upd = d if upd is None else upd + d
            _build_gates(*gate_refs, nxt_row0, nxt_scr, bounds[q], bounds[q + 1])
            q += 1
        o_ref[...] += upd

    odd_row0 = keys_per_step % ROW_BLOCK

    @pl.when(lax.rem(j, 2) == 0)
    def _():
        step(g0_scr, g1_scr, odd_row0)

    @pl.when(lax.rem(j, 2) == 1)
    def _():
        step(g1_scr, g0_scr, 0)


def _peer_dense_call(h2t, u_bf, vt_bf, r2, e2, nsel, e1, tm, te):
    d, t = h2t.shape
    ne = u_bf.shape[0]
    nj = ne // te
    keys_per_step = te // N_KEYS
    tiles_per_block = ROW_BLOCK // keys_per_step
    assert nj % 2 == 0 and tiles_per_block in (1, 2), "a sublane-aligned 8-row block of keys spans one or two tiles"
    st = pl.BlockSpec((PEER_HEADS, N_KEYS, tm), lambda i, j: (0, 0, i))
    nxt_rows = pl.BlockSpec((PEER_HEADS, ROW_BLOCK, tm), lambda i, j: (0, lax.rem(j + 1, nj) // tiles_per_block, i))
    cur_rows = pl.BlockSpec((PEER_HEADS, ROW_BLOCK, tm), lambda i, j: (0, 0, i))
    return pl.pallas_call(
        functools.partial(_peer_dense_kernel, te=te),
        grid=(t // tm, nj),
        in_specs=[
            pl.BlockSpec((d, tm), lambda i, j: (0, i)),
            pl.BlockSpec((te, d), lambda i, j: (j, 0)),
            pl.BlockSpec((d, te), lambda i, j: (0, j)),
            st, st, nxt_rows, nxt_rows, cur_rows, cur_rows,
        ],
        out_specs=pl.BlockSpec((d, tm), lambda i, j: (0, i)),
        out_shape=SDS((d, t), F32),
        scratch_shapes=[pltpu.VMEM((te, tm), BF16), pltpu.VMEM((te, tm), BF16)],
        compiler_params=_cp("parallel", "arbitrary"),
        name="peer_dense",
    )(h2t, u_bf, vt_bf, r2, e2, nsel, e1, nsel, e1)


def _rope_tables(pos):
    half = HEAD_DIM // 2
    inv = ROPE_THETA ** (-jnp.arange(half, dtype=F32) / half)
    ang = pos.astype(F32)[:, None] * inv[None, :]
    cos, sin = jnp.cos(ang), jnp.sin(ang)
    return jnp.concatenate([cos, cos], axis=1), jnp.concatenate([-sin, sin], axis=1)


def _layer_weights(l, w_in, b_f, w_o_fox, w_o_diff, w_out, peer_w_q, peer_u, peer_v):
    wl = w_in[l]
    c_fl = 3 * FOX_W
    w_r = jnp.concatenate(
        [wl[:, :c_fl], wl[:, c_fl + FOX_HEADS:], jnp.pad(wl[:, c_fl:c_fl + FOX_HEADS], ((0, 0), (0, LANES - FOX_HEADS)))],
        axis=1).astype(BF16)
    bf_pad = jnp.pad(b_f[l][None, :], ((0, 0), (0, LANES - FOX_HEADS)))
    return dict(
        w_r=w_r, bf_pad=bf_pad,
        wf=w_o_fox[l].astype(BF16), wd=w_o_diff[l].astype(BF16), wo=w_out[l].astype(BF16),
        pqt=peer_w_q[l].T.astype(BF16), u=peer_u[l].astype(BF16), vt=peer_v[l].T.astype(BF16),
    )


def _tail(x2d, mods, proj, fo, dn, lw, vecs, alpha, tm, peer_tm, peer_te):
    _, _, gate1, shift2, scale2, gate2 = mods
    t = x2d.shape[0]
    y = _mix_call(fo, dn, proj, lw["wf"], lw["wd"], tm)
    x1 = _resid_ln_call(x2d, y, gate1, lw["wo"], vecs["ln1_g"], vecs["ln1_b"], tm, alpha)
    tp = -(-t // peer_tm) * peer_tm
    pad = tp - t
    x1p = jnp.pad(x1, ((0, pad), (0, 0))) if pad else x1
    sc2, sh2 = scale2, shift2
    if pad and sc2.shape[0] != 1:
        sc2 = jnp.pad(sc2, ((0, pad), (0, 0)))
        sh2 = jnp.pad(sh2, ((0, pad), (0, 0)))
    h2t, qt = _peer_q_call(x1p, sc2, sh2, lw["pqt"], min(peer_tm, 256))
    r2, e2, nsel, e1 = _peer_sel_call(qt, vecs["pk1"], vecs["pk2"], peer_tm)
    ffn_t = _peer_dense_call(h2t, lw["u"], lw["vt"], r2, e2, nsel, e1, peer_tm, peer_te)
    if pad:
        ffn_t = ffn_t[:, :t]
    return _final_ln_call(x1, ffn_t, gate2, vecs["ln2_g"], vecs["ln2_b"], tm, alpha)


def kernel(x_prompt, x_sample, cache_fox_k, cache_fox_v, cache_fox_logf, cache_diff_k, cache_diff_v, page_table, c_prompt, c_sample, w_ada, b_ada, w_in, b_f, lambda_q1, lambda_k1, lambda_q2, lambda_k2, diff_subln_w, w_o_fox, w_o_diff, w_out, ln1_g, ln1_b, peer_w_q, peer_sub_k1, peer_sub_k2, peer_u, peer_v, ln2_g, ln2_b):
    depth = w_ada.shape[0]
    bp, sp, d = x_prompt.shape
    bs, ss, _ = x_sample.shape
    assert bp == 1 and ss == 1, "one prompt sequence and one new token per sample sequence"
    n_pages = page_table.shape[1]
    page = cache_fox_k.shape[2]
    n_pool = cache_fox_k.shape[1]
    n_past = n_pages * page
    alpha = (2 * depth) ** 0.25
    dv = DIFF_V
    pg = 16

    cos_p, sin_p = _rope_tables(jnp.arange(sp))
    cos_s, sin_s = _rope_tables(n_past + jnp.arange(ss))
    lf_flat = cache_fox_logf.reshape(depth, n_pool, 1, page * FOX_HEADS)

    c_all = jnp.concatenate([c_prompt, c_sample], axis=0)
    n_c = c_all.shape[0]
    c_all = jnp.pad(c_all, ((0, -n_c % 8), (0, 0)))

    xp = x_prompt.reshape(sp, d)
    xs = x_sample.reshape(bs, d)
    st_p, st_s = [], []
    for l in range(depth):
        lam_init = 0.8 - 0.6 * math.exp(-0.3 * l)
        lw = _layer_weights(l, w_in, b_f, w_o_fox, w_o_diff, w_out, peer_w_q, peer_u, peer_v)
        vecs = dict(ln1_g=ln1_g[l][None], ln1_b=ln1_b[l][None], ln2_g=ln2_g[l][None], ln2_b=ln2_b[l][None],
                    pk1=peer_sub_k1[l], pk2=peer_sub_k2[l])
        lam_vecs = [v[l][None] for v in (lambda_q1, lambda_k1, lambda_q2, lambda_k2)]
        subln = diff_subln_w[l][None]
        subln_rep = jnp.broadcast_to(diff_subln_w[l][:, None], (dv, LANES))

        mod = _ada_call(c_all, w_ada, b_ada, l)
        mods_p = [mod[0:bp, i * d:(i + 1) * d] for i in range(6)]
        mods_s = [mod[bp:bp + bs, i * d:(i + 1) * d] for i in range(6)]

        proj = _lnmod_mm_call(xp, mods_p[1], mods_p[0], lw["w_r"], 1024, 1152)
        (fk_o, fv_o, dk_o, dv_o, logf, fqt, fkb, fvt, dqt, dkb, dvt, ckb, cumt) = _post_prompt_call(
            proj, cos_p, sin_p, lw["bf_pad"], 256)
        fo = _fox_flash_call(fqt, fkb, fvt, cumt, ckb, 512)
        dn = _diff_flash_call(dqt, dkb, dvt, lam_vecs, subln_rep, 512, lam_init)
        st_p.append((fk_o.reshape(bp, sp, FOX_HEADS, HEAD_DIM), fv_o.reshape(bp, sp, FOX_HEADS, HEAD_DIM),
                     logf.reshape(bp, sp, FOX_HEADS),
                     dk_o.reshape(bp, sp, DIFF_HEADS, dv), dv_o.reshape(bp, sp, DIFF_HEADS, dv)))
        xp = _tail(xp, mods_p, proj, fo, dn, lw, vecs, alpha, 256, 512, 1024)

        proj_s = _lnmod_mm_call(xs, mods_s[1], mods_s[0], lw["w_r"], bs, 1152)
        dq_s, dk_s, logf_s = _post_call(proj_s, cos_s, sin_s, lw["bf_pad"], bs)
        fk_s = proj_s[:, FOX_W:2 * FOX_W].reshape(bs, FOX_HEADS, HEAD_DIM)
        fv_s = proj_s[:, 2 * FOX_W:3 * FOX_W].reshape(bs, FOX_HEADS, HEAD_DIM)
        dv_s = proj_s[:, 5 * FOX_W:6 * FOX_W].reshape(bs, DIFF_HEADS, dv)
        lfn_cls = jnp.tile(logf_s, (1, LANES // FOX_HEADS)).reshape(bs, 1, LANES)
        fo_s = _fox_dec_call(l, page_table, proj_s[:, :FOX_W].reshape(bs, FOX_HEADS, HEAD_DIM), fk_s, fv_s, lfn_cls,
                             cache_fox_k, cache_fox_v, lf_flat, pg)
        dn_s = _diff_dec_call(l, page_table, dq_s.reshape(bs, N_MAPS, HEAD_DIM), dk_s.reshape(bs, N_MAPS, HEAD_DIM),
                              jnp.repeat(dv_s, 2, axis=1), lam_vecs, subln, cache_diff_k, cache_diff_v, pg, lam_init)
        dn_s = dn_s[:, ::2, :].reshape(bs, FOX_W)
        st_s.append((fk_s.reshape(bs, ss, FOX_HEADS, HEAD_DIM), fv_s.reshape(bs, ss, FOX_HEADS, HEAD_DIM),
                     logf_s.reshape(bs, ss, FOX_HEADS), dk_s.reshape(bs, ss, DIFF_HEADS, dv),
                     dv_s.reshape(bs, ss, DIFF_HEADS, dv)))
        xs = _tail(xs, mods_s, proj_s, fo_s.reshape(bs, FOX_W), dn_s, lw, vecs, alpha, bs, LANES, 512)

    outs_p = [jnp.stack([s[i] for s in st_p]) for i in range(5)]
    outs_s = [jnp.stack([s[i] for s in st_s]) for i in range(5)]
    return (xp.reshape(bp, sp, d), xs.reshape(bs, ss, d), *outs_p, *outs_s)
```

```python
import functools
import math

import jax
import jax.numpy as jnp
import numpy as np
from jax import lax
from jax.experimental import pallas as pl
from jax.experimental.pallas import tpu as pltpu

F32 = jnp.float32
BF16 = jnp.bfloat16
NEG_INF = float("-inf")
POS_INF = float("inf")

LN_EPS = 1e-5
ROPE_THETA = 10000.0
HEAD_DIM = 128
FOX_HEADS = 8
DIFF_HEADS = 4
N_MAPS = 8
FOX_W = FOX_HEADS * HEAD_DIM
DIFF_V = 2 * HEAD_DIM
PEER_HEADS = 8
PEER_TOPK = 16
N_KEYS = 128
ATTN_SCALE = HEAD_DIM ** -0.5
LOG2E = 1.0 / math.log(2.0)
QK_SCALE_LOG2 = ATTN_SCALE * LOG2E
LANES = 128
MXU_DIM = 256
VMEM_LIMIT = 48 * 1024 * 1024

SDS = jax.ShapeDtypeStruct


def _cp(*sem):
    return pltpu.CompilerParams(dimension_semantics=sem, vmem_limit_bytes=VMEM_LIMIT)


def _ln(x):
    mu = jnp.mean(x, axis=-1, keepdims=True)
    xc = x - mu
    var = jnp.mean(xc * xc, axis=-1, keepdims=True)
    return xc * lax.rsqrt(var + LN_EPS)


def _dot_nt(a, b):
    return lax.dot_general(a, b, (((1,), (1,)), ((), ())), preferred_element_type=F32)


def _dot(a, b):
    return jnp.dot(a, b, preferred_element_type=F32)


def _rep_chunks(x, k):
    return x if k == 1 else jnp.concatenate([x] * k, axis=1)


def _ada_kernel(c_ref, w_ref, b_ref, o_ref):
    c = c_ref[...]
    a = (c * jax.nn.sigmoid(c)).astype(BF16)
    o_ref[...] = _dot(a, w_ref[...].astype(BF16)) + b_ref[...]


def _ada_call(c_all, w_ada, b_ada, layer):
    mp, d = c_all.shape
    n = w_ada.shape[2]
    tn = 1024
    return pl.pallas_call(
        _ada_kernel,
        grid=(n // tn,),
        in_specs=[
            pl.BlockSpec((mp, d), lambda j: (0, 0)),
            pl.BlockSpec((None, d, tn), lambda j: (layer, 0, j)),
            pl.BlockSpec((None, 1, tn), lambda j: (layer, 0, j)),
        ],
        out_specs=pl.BlockSpec((mp, tn), lambda j: (0, j)),
        out_shape=SDS((mp, n), F32),
        compiler_params=_cp("arbitrary"),
        name="ada",
    )(c_all, w_ada, b_ada.reshape(b_ada.shape[0], 1, n))


def _lnmod_mm_kernel(x_ref, sc_ref, sh_ref, w_ref, o_ref, h_scr):
    @pl.when(pl.program_id(1) == 0)
    def _():
        h = _ln(x_ref[...]) * (1.0 + sc_ref[...]) + sh_ref[...]
        h_scr[...] = h.astype(BF16)

    o_ref[...] = _dot(h_scr[...], w_ref[...])


def _mod_spec2(r, tm, d):
    return pl.BlockSpec((1, d), lambda i, j: (0, 0)) if r == 1 else pl.BlockSpec((tm, d), lambda i, j: (i, 0))


def _lnmod_mm_call(x, sc, sh, w, tm, tn):
    t, d = x.shape
    n = w.shape[1]
    mod_spec = _mod_spec2(sc.shape[0], tm, d)
    return pl.pallas_call(
        _lnmod_mm_kernel,
        grid=(t // tm, n // tn),
        in_specs=[
            pl.BlockSpec((tm, d), lambda i, j: (i, 0)),
            mod_spec,
            mod_spec,
            pl.BlockSpec((d, tn), lambda i, j: (0, j)),
        ],
        out_specs=pl.BlockSpec((tm, tn), lambda i, j: (i, j)),
        out_shape=SDS((t, n), F32),
        scratch_shapes=[pltpu.VMEM((tm, d), BF16)],
        compiler_params=_cp("parallel", "arbitrary"),
        name="lnmod_mm",
    )(x, sc, sh, w)


def _rope(x, c, s):
    return x * c + pltpu.roll(x, HEAD_DIM // 2, axis=1) * s


def _log_sigmoid(z):
    return jnp.minimum(z, 0.0) - jnp.log1p(jnp.exp(-jnp.abs(z)))


def _post_kernel(dq_ref, dk_ref, fl_ref, cos_ref, sin_ref, bf_ref, dqo_ref, dko_ref, lf_ref):
    c = cos_ref[...]
    s = sin_ref[...]
    for j in range(N_MAPS):
        sl = slice(j * HEAD_DIM, (j + 1) * HEAD_DIM)
        dqo_ref[:, sl] = _rope(dq_ref[:, sl], c, s)
        dko_ref[:, sl] = _rope(dk_ref[:, sl], c, s)
    lf_ref[...] = _log_sigmoid(fl_ref[...] + bf_ref[...])[:, :FOX_HEADS]


def _post_call(proj, cos2, sin2, bf_pad, tm):
    t = proj.shape[0]
    tab_spec = pl.BlockSpec((1, LANES), lambda i: (0, 0))
    row = pl.BlockSpec((tm, FOX_W), lambda i: (i, 0))
    return pl.pallas_call(
        _post_kernel,
        grid=(t // tm,),
        in_specs=[
            pl.BlockSpec((tm, FOX_W), lambda i: (i, 3)),
            pl.BlockSpec((tm, FOX_W), lambda i: (i, 4)),
            pl.BlockSpec((tm, LANES), lambda i: (i, 80)),
            tab_spec, tab_spec, tab_spec,
        ],
        out_specs=[row, row, pl.BlockSpec((tm, FOX_HEADS), lambda i: (i, 0))],
        out_shape=[SDS((t, FOX_W), F32), SDS((t, FOX_W), F32), SDS((t, FOX_HEADS), F32)],
        compiler_params=_cp("arbitrary"),
        name="post_proj",
    )(proj, proj, proj, cos2, sin2, bf_pad)


def _post_prompt_kernel(fq_ref, fk_ref, fv_ref, dq_ref, dk_ref, dv_ref, fl_ref, cos_ref, sin_ref, bf_ref,
                        fko_ref, fvo_ref, dko_ref, dvo_ref, lf_ref,
                        fqt_ref, fkb_ref, fvt_ref, dqt_ref, dkb_ref, dvt_ref, ckb_ref, cumt_ref,
                        carry_scr, *, tm):
    c = cos_ref[...]
    s = sin_ref[...]
    for j in range(N_MAPS):
        sl = slice(j * HEAD_DIM, (j + 1) * HEAD_DIM)
        half = slice((j % 2) * HEAD_DIM, (j % 2 + 1) * HEAD_DIM)
        dk = _rope(dk_ref[:, sl], c, s)
        dko_ref[:, j // 2, half] = dk
        dkb_ref[:, sl] = dk.astype(BF16)
        dqt_ref[sl, :] = (_rope(dq_ref[:, sl], c, s) * QK_SCALE_LOG2).T.astype(BF16)
        fqt_ref[sl, :] = (fq_ref[:, sl] * QK_SCALE_LOG2).T.astype(BF16)
        fk = fk_ref[:, sl]
        fko_ref[:, j, :] = fk
        fkb_ref[:, sl] = fk.astype(BF16)
        fv = fv_ref[:, sl]
        fvo_ref[:, j, :] = fv
        fvt_ref[sl, :] = fv.T.astype(BF16)
        dv = dv_ref[:, sl]
        dvo_ref[:, j // 2, half] = dv
        dvt_ref[sl, :] = dv.T.astype(BF16)
    lf = _log_sigmoid(fl_ref[...] + bf_ref[...])
    lf_ref[...] = lf[:, :FOX_HEADS]

    @pl.when(pl.program_id(0) == 0)
    def _():
        carry_scr[...] = jnp.zeros_like(carry_scr)

    row = lax.broadcasted_iota(jnp.int32, (tm, tm), 0)
    col = lax.broadcasted_iota(jnp.int32, (tm, tm), 1)
    tri = jnp.where(col <= row, 1.0, 0.0).astype(BF16)
    hi = lf.astype(BF16)
    r1 = lf - hi.astype(F32)
    mid = r1.astype(BF16)
    lo = (r1 - mid.astype(F32)).astype(BF16)
    cum = (_dot(tri, hi) + _dot(tri, mid) + _dot(tri, lo)) + carry_scr[...]
    carry_scr[...] = cum[tm - 1:tm, :]
    cum2 = cum * LOG2E
    cumt_ref[...] = cum2.T[:FOX_HEADS, :]
    for h in range(FOX_HEADS):
        ckb_ref[h] = jnp.broadcast_to(cum2[:, h:h + 1], (tm, LANES))


def _post_prompt_call(proj, cos2, sin2, bf_pad, tm):
    t = proj.shape[0]
    col = lambda c: pl.BlockSpec((tm, FOX_W), lambda i: (i, c))
    tab = pl.BlockSpec((tm, LANES), lambda i: (i, 0))
    row_f32 = pl.BlockSpec((tm, FOX_W), lambda i: (i, 0))
    tr = pl.BlockSpec((FOX_W, tm), lambda i: (0, i))
    fox_st = pl.BlockSpec((tm, FOX_HEADS, HEAD_DIM), lambda i: (i, 0, 0))
    diff_st = pl.BlockSpec((tm, DIFF_HEADS, DIFF_V), lambda i: (i, 0, 0))
    return pl.pallas_call(
        functools.partial(_post_prompt_kernel, tm=tm),
        grid=(t // tm,),
        in_specs=[col(0), col(1), col(2), col(3), col(4), col(5),
                  pl.BlockSpec((tm, LANES), lambda i: (i, 80)), tab, tab,
                  pl.BlockSpec((1, LANES), lambda i: (0, 0))],
        out_specs=[fox_st, fox_st, diff_st, diff_st, pl.BlockSpec((tm, FOX_HEADS), lambda i: (i, 0)),
                   tr, row_f32, tr, tr, row_f32, tr,
                   pl.BlockSpec((FOX_HEADS, tm, LANES), lambda i: (0, i, 0)),
                   pl.BlockSpec((FOX_HEADS, tm), lambda i: (0, i))],
        out_shape=[SDS((t, FOX_HEADS, HEAD_DIM), F32), SDS((t, FOX_HEADS, HEAD_DIM), F32),
                   SDS((t, DIFF_HEADS, DIFF_V), F32), SDS((t, DIFF_HEADS, DIFF_V), F32), SDS((t, FOX_HEADS), F32),
                   SDS((FOX_W, t), BF16), SDS((t, FOX_W), BF16), SDS((FOX_W, t), BF16),
                   SDS((FOX_W, t), BF16), SDS((t, FOX_W), BF16), SDS((FOX_W, t), BF16),
                   SDS((FOX_HEADS, t, LANES), F32), SDS((FOX_HEADS, t), F32)],
        scratch_shapes=[pltpu.VMEM((1, LANES), F32)],
        compiler_params=_cp("arbitrary"),
        name="post_prompt",
    )(proj, proj, proj, proj, proj, proj, proj, cos2, sin2, bf_pad)


def _tri_pairs(n):
    qi = np.concatenate([np.full(i + 1, i, np.int32) for i in range(n)])
    ki = np.concatenate([np.arange(i + 1, dtype=np.int32) for i in range(n)])
    return jnp.asarray(qi), jnp.asarray(ki)


def _flash_update(st, vt_bf, m_scr, l_scr, acc_scr, idx, rows):
    m_prev = m_scr[idx:idx + 1, :]
    m_new = jnp.maximum(m_prev, jnp.max(st, axis=0, keepdims=True))
    alpha = jnp.exp2(m_prev - m_new)
    p = jnp.exp2(st - m_new)
    l_scr[idx:idx + 1, :] = alpha * l_scr[idx:idx + 1, :] + jnp.sum(p, axis=0, keepdims=True)
    acc_scr[rows, :] = alpha * acc_scr[rows, :] + _dot(vt_bf, p.astype(BF16))
    m_scr[idx:idx + 1, :] = m_new


def _flash_init(m_scr, l_scr, acc_scr):
    m_scr[...] = jnp.full_like(m_scr, NEG_INF)
    l_scr[...] = jnp.zeros_like(l_scr)
    acc_scr[...] = jnp.zeros_like(acc_scr)


def _causal_keep_t(tk, tq):
    kpos = lax.broadcasted_iota(jnp.int32, (tk, tq), 0)
    qpos = lax.broadcasted_iota(jnp.int32, (tk, tq), 1)
    return kpos <= qpos


def _fox_flash_kernel(qi_tab, ki_tab, qt_ref, k_ref, vt_ref, cq_ref, ckb_ref, o_ref, m_scr, l_scr, acc_scr, *, tq, tk):
    pid = pl.program_id(0)
    qi = qi_tab[pid]
    ki = ki_tab[pid]

    @pl.when(ki == 0)
    def _():
        _flash_init(m_scr, l_scr, acc_scr)

    def body(masked):
        keep = _causal_keep_t(tk, tq) if masked else None
        for h in range(FOX_HEADS):
            rows = slice(h * HEAD_DIM, (h + 1) * HEAD_DIM)
            st = _dot(k_ref[:, rows], qt_ref[rows, :])
            st = st + (cq_ref[h:h + 1, :] - _rep_chunks(ckb_ref[h], tq // LANES))
            if masked:
                st = jnp.where(keep, st, NEG_INF)
            _flash_update(st, vt_ref[rows, :], m_scr, l_scr, acc_scr, h, rows)

    @pl.when(ki < qi)
    def _():
        body(False)

    @pl.when(ki == qi)
    def _():
        body(True)
        for h in range(FOX_HEADS):
            rows = slice(h * HEAD_DIM, (h + 1) * HEAD_DIM)
            o_ref[:, rows] = (acc_scr[rows, :] / l_scr[h:h + 1, :]).T.astype(o_ref.dtype)


def _fox_flash_call(fqt, fkb, fvt, cumt, ckb, tq):
    t = fkb.shape[0]
    nq = t // tq
    qi_tab, ki_tab = _tri_pairs(nq)
    grid_spec = pltpu.PrefetchScalarGridSpec(
        num_scalar_prefetch=2,
        grid=(int(qi_tab.shape[0]),),
        in_specs=[
            pl.BlockSpec((FOX_W, tq), lambda p, qt, kt: (0, qt[p])),
            pl.BlockSpec((tq, FOX_W), lambda p, qt, kt: (kt[p], 0)),
            pl.BlockSpec((FOX_W, tq), lambda p, qt, kt: (0, kt[p])),
            pl.BlockSpec((FOX_HEADS, tq), lambda p, qt, kt: (0, qt[p])),
            pl.BlockSpec((FOX_HEADS, tq, LANES), lambda p, qt, kt: (0, kt[p], 0)),
        ],
        out_specs=pl.BlockSpec((tq, FOX_W), lambda p, qt, kt: (qt[p], 0)),
        scratch_shapes=[
            pltpu.VMEM((N_MAPS, tq), F32),
            pltpu.VMEM((N_MAPS, tq), F32),
            pltpu.VMEM((FOX_W, tq), F32),
        ],
    )
    return pl.pallas_call(
        functools.partial(_fox_flash_kernel, tq=tq, tk=tq),
        grid_spec=grid_spec,
        out_shape=SDS((t, FOX_W), BF16),
        compiler_params=_cp("arbitrary"),
        name="fox_flash",
    )(qi_tab, ki_tab, fqt, fkb, fvt, cumt, ckb)


def _lambda_full(lq1_ref, lk1_ref, lq2_ref, lk2_ref, lam_init):
    a = jnp.exp(jnp.sum(lq1_ref[...] * lk1_ref[...], axis=1, keepdims=True))
    b = jnp.exp(jnp.sum(lq2_ref[...] * lk2_ref[...], axis=1, keepdims=True))
    return a - b + lam_init


def _diff_flash_kernel(qi_tab, ki_tab, qt_ref, k_ref, vt_ref, lq1_ref, lk1_ref, lq2_ref, lk2_ref, sub_ref,
                       o_ref, m_scr, l_scr, acc_scr, *, tq, tk, lam_init):
    pid = pl.program_id(0)
    qi = qi_tab[pid]
    ki = ki_tab[pid]

    @pl.when(ki == 0)
    def _():
        _flash_init(m_scr, l_scr, acc_scr)

    def body(masked):
        keep = _causal_keep_t(tk, tq) if masked else None
        for mp in range(N_MAPS):
            hd = mp // 2
            rows = slice(mp * HEAD_DIM, (mp + 1) * HEAD_DIM)
            st = _dot(k_ref[:, rows], qt_ref[rows, :])
            if masked:
                st = jnp.where(keep, st, NEG_INF)
            _flash_update(st, vt_ref[hd * DIFF_V:(hd + 1) * DIFF_V, :], m_scr, l_scr, acc_scr, mp,
                          slice(mp * DIFF_V, (mp + 1) * DIFF_V))

    @pl.when(ki < qi)
    def _():
        body(False)

    @pl.when(ki == qi)
    def _():
        body(True)
        lam = _lambda_full(lq1_ref, lk1_ref, lq2_ref, lk2_ref, lam_init)
        sub = _rep_chunks(sub_ref[...], tq // LANES)
        for hd in range(DIFF_HEADS):
            o1 = acc_scr[(2 * hd) * DIFF_V:(2 * hd + 1) * DIFF_V, :] / l_scr[2 * hd:2 * hd + 1, :]
            o2 = acc_scr[(2 * hd + 1) * DIFF_V:(2 * hd + 2) * DIFF_V, :] / l_scr[2 * hd + 1:2 * hd + 2, :]
            d = o1 - lam * o2
            y = d * lax.rsqrt(jnp.mean(d * d, axis=0, keepdims=True) + LN_EPS)
            o_ref[:, hd * DIFF_V:(hd + 1) * DIFF_V] = ((y * sub) * (1.0 - lam_init)).T.astype(o_ref.dtype)


def _diff_flash_call(dqt, dkb, dvt, lam_vecs, subln_rep, tq, lam_init):
    t = dkb.shape[0]
    nq = t // tq
    qi_tab, ki_tab = _tri_pairs(nq)
    vec = pl.BlockSpec((1, HEAD_DIM), lambda p, qt, kt: (0, 0))
    grid_spec = pltpu.PrefetchScalarGridSpec(
        num_scalar_prefetch=2,
        grid=(int(qi_tab.shape[0]),),
        in_specs=[
            pl.BlockSpec((FOX_W, tq), lambda p, qt, kt: (0, qt[p])),
            pl.BlockSpec((tq, FOX_W), lambda p, qt, kt: (kt[p], 0)),
            pl.BlockSpec((FOX_W, tq), lambda p, qt, kt: (0, kt[p])),
            vec, vec, vec, vec,
            pl.BlockSpec((DIFF_V, LANES), lambda p, qt, kt: (0, 0)),
        ],
        out_specs=pl.BlockSpec((tq, FOX_W), lambda p, qt, kt: (qt[p], 0)),
        scratch_shapes=[
            pltpu.VMEM((N_MAPS, tq), F32),
            pltpu.VMEM((N_MAPS, tq), F32),
            pltpu.VMEM((N_MAPS * DIFF_V, tq), F32),
        ],
    )
    return pl.pallas_call(
        functools.partial(_diff_flash_kernel, tq=tq, tk=tq, lam_init=lam_init),
        grid_spec=grid_spec,
        out_shape=SDS((t, FOX_W), BF16),
        compiler_params=_cp("arbitrary"),
        name="diff_flash",
    )(qi_tab, ki_tab, dqt, dkb, dvt, *lam_vecs, subln_rep)


def _class_allreduce(x, op, period):
    s = period
    while s < LANES:
        x = op(x, pltpu.roll(x, s, axis=1))
        s *= 2
    return x


def _fold_chunks(x, op):
    out = x[:, :LANES]
    for c in range(1, x.shape[1] // LANES):
        out = op(out, x[:, c * LANES:(c + 1) * LANES])
    return out


def _suffix_incl(x, period):
    n = x.shape[1]
    lane = lax.broadcasted_iota(jnp.int32, x.shape, 1)
    s = period
    while s < n:
        y = pltpu.roll(x, n - s, axis=1)
        x = x + jnp.where(lane < n - s, y, 0.0)
        s *= 2
    return x


def _fox_dec_kernel(pt_ref, q_ref, kn_ref, vn_ref, lfn_ref, *refs, pg, nj):
    del pt_ref
    k_refs, v_refs, lf_refs = refs[:pg], refs[pg:2 * pg], refs[2 * pg:3 * pg]
    o_ref = refs[3 * pg]
    m_scr, l_scr, acc_scr, carry_scr = refs[3 * pg + 1:]
    j = pl.program_id(1)
    nh = FOX_HEADS
    w = PAGE_ROWS_FOX
    sub = lax.broadcasted_iota(jnp.int32, (nh, w), 0)
    lane = lax.broadcasted_iota(jnp.int32, (nh, w), 1)
    diag = sub == (lane & (nh - 1))
    diag1 = diag[:, :LANES]

    def to_sub(x_cls):
        return jnp.max(jnp.where(diag1, jnp.broadcast_to(x_cls, (nh, LANES)), NEG_INF), axis=1, keepdims=True)

    @pl.when(j == 0)
    def _():
        m_scr[...] = jnp.full_like(m_scr, NEG_INF)
        l_scr[...] = jnp.zeros_like(l_scr)
        acc_scr[...] = jnp.zeros_like(acc_scr)
        carry_scr[...] = lfn_ref[...]

    qb = (q_ref[...] * ATTN_SCALE).astype(BF16)

    lf = jnp.concatenate([r[...] for r in lf_refs], axis=0)
    tot = _class_allreduce(_fold_chunks(lf, jnp.add), jnp.add, nh)
    run = carry_scr[...]
    before = []
    for i in range(pg):
        before.append(run)
        run = run + tot[i:i + 1, :]
    carry_scr[...] = run
    bias = (_suffix_incl(lf, nh) - lf) + _rep_chunks(jnp.concatenate(before, axis=0), w // LANES)

    rows = []
    for i in range(pg):
        k2 = k_refs[i][...].reshape(w, HEAD_DIM).astype(BF16)
        st = _dot_nt(qb, k2)
        rows.append(jnp.sum(jnp.where(diag, st, 0.0), axis=0, keepdims=True))
    s = jnp.concatenate(rows, axis=0) + bias

    mc = _class_allreduce(_fold_chunks(jnp.max(s, axis=0, keepdims=True), jnp.maximum), jnp.maximum, nh)
    m_old = m_scr[...]
    m_new = jnp.maximum(m_old, mc)
    alpha = jnp.exp(m_old - m_new)
    p = jnp.exp(s - _rep_chunks(m_new, w // LANES))
    l_scr[...] = l_scr[...] * _rep_chunks(alpha, w // LANES) + p

    acc_c = jnp.zeros((nh, HEAD_DIM), F32)
    for i in range(pg):
        pm = jnp.where(diag, jnp.broadcast_to(p[i:i + 1, :], (nh, w)), 0.0).astype(BF16)
        v2 = v_refs[i][...].reshape(w, HEAD_DIM).astype(BF16)
        acc_c = acc_c + _dot(pm, v2)
    acc_scr[...] = acc_scr[...] * to_sub(alpha) + acc_c
    m_scr[...] = m_new

    @pl.when(j == nj - 1)
    def _():
        kn = kn_ref[...].astype(BF16).astype(F32)
        s_self = jnp.sum(qb.astype(F32) * kn, axis=1, keepdims=True)
        s_cls = jnp.sum(jnp.where(diag1, jnp.broadcast_to(s_self, (nh, LANES)), 0.0), axis=0, keepdims=True)
        m_fin = jnp.maximum(m_new, s_cls)
        a2 = jnp.exp(m_new - m_fin)
        p_self = jnp.exp(s_cls - m_fin)
        l_cls = _class_allreduce(_fold_chunks(jnp.sum(l_scr[...], axis=0, keepdims=True), jnp.add), jnp.add, nh)
        l_cls = l_cls * a2 + p_self
        o = (acc_scr[...] * to_sub(a2) + to_sub(p_self) * vn_ref[...]) / to_sub(l_cls)
        o_ref[...] = o


PAGE_ROWS_FOX = 128 * FOX_HEADS
PAGE_ROWS_DIFF = 128 * DIFF_HEADS


def _fox_dec_call(layer, page_table, q, k_new, v_new, lfn_cls, cache_k, cache_v, cache_lf_flat, pg):
    b = q.shape[0]
    n_pages = page_table.shape[1]
    page = cache_k.shape[2]
    nj = n_pages // pg

    def page_idx(i):
        return lambda bb, j, pt: (layer, pt[bb, n_pages - 1 - (j * pg + i)], 0, 0, 0)

    def lf_idx(i):
        return lambda bb, j, pt: (layer, pt[bb, n_pages - 1 - (j * pg + i)], 0, 0)

    tok = pl.BlockSpec((None, FOX_HEADS, HEAD_DIM), lambda bb, j, pt: (bb, 0, 0))
    in_specs = [tok, tok, tok, pl.BlockSpec((None, 1, LANES), lambda bb, j, pt: (bb, 0, 0))]
    in_specs += [pl.BlockSpec((None, None, page, FOX_HEADS, HEAD_DIM), page_idx(i)) for i in range(pg)]
    in_specs += [pl.BlockSpec((None, None, page, FOX_HEADS, HEAD_DIM), page_idx(i)) for i in range(pg)]
    in_specs += [pl.BlockSpec((None, None, 1, page * FOX_HEADS), lf_idx(i)) for i in range(pg)]
    grid_spec = pltpu.PrefetchScalarGridSpec(
        num_scalar_prefetch=1,
        grid=(b, nj),
        in_specs=in_specs,
        out_specs=pl.BlockSpec((None, FOX_HEADS, HEAD_DIM), lambda bb, j, pt: (bb, 0, 0)),
        scratch_shapes=[
            pltpu.VMEM((1, LANES), F32),
            pltpu.VMEM((pg, page * FOX_HEADS), F32),
            pltpu.VMEM((FOX_HEADS, HEAD_DIM), F32),
            pltpu.VMEM((1, LANES), F32),
        ],
    )
    return pl.pallas_call(
        functools.partial(_fox_dec_kernel, pg=pg, nj=nj),
        grid_spec=grid_spec,
        out_shape=SDS((b, FOX_HEADS, HEAD_DIM), F32),
        compiler_params=_cp("parallel", "arbitrary"),
        name="fox_decode",
    )(page_table, q, k_new, v_new, lfn_cls, *([cache_k] * pg), *([cache_v] * pg), *([cache_lf_flat] * pg))


def _diff_dec_kernel(pt_ref, q_ref, kn_ref, vn_ref, lq1_ref, lk1_ref, lq2_ref, lk2_ref, sub_ref, *refs,
                     pg, nj, lam_init):
    del pt_ref
    k_refs, v_refs = refs[:pg], refs[pg:2 * pg]
    o_ref = refs[2 * pg]
    m_scr, l_scr, acc_scr = refs[2 * pg + 1:]
    j = pl.program_id(1)
    nh = DIFF_HEADS
    w = PAGE_ROWS_DIFF
    dv = DIFF_V
    sub = lax.broadcasted_iota(jnp.int32, (N_MAPS, w), 0)
    lane = lax.broadcasted_iota(jnp.int32, (N_MAPS, w), 1)
    same_head = (sub >> 1) == (lane & (nh - 1))
    sel = [jnp.logical_and(same_head, (sub & 1) == n) for n in range(2)]
    sel1 = [x[:, :LANES] for x in sel]

    def to_sub(x0, x1):
        y = jnp.where(sel1[0], jnp.broadcast_to(x0, (N_MAPS, LANES)),
                      jnp.where(sel1[1], jnp.broadcast_to(x1, (N_MAPS, LANES)), NEG_INF))
        return jnp.max(y, axis=1, keepdims=True)

    @pl.when(j == 0)
    def _():
        m_scr[...] = jnp.full_like(m_scr, NEG_INF)
        l_scr[...] = jnp.zeros_like(l_scr)
        acc_scr[...] = jnp.zeros_like(acc_scr)

    q8 = q_ref[...] * ATTN_SCALE
    odd = (lax.broadcasted_iota(jnp.int32, (N_MAPS, HEAD_DIM), 0) & 1) == 1
    qd = jnp.concatenate([jnp.where(odd, 0.0, q8), jnp.where(odd, q8, 0.0)], axis=1).astype(BF16)

    rows = [[], []]
    for i in range(pg):
        k2 = k_refs[i][...].reshape(w, dv).astype(BF16)
        st = _dot_nt(qd, k2)
        for n in range(2):
            rows[n].append(jnp.sum(jnp.where(sel[n], st, 0.0), axis=0, keepdims=True))
    s = [jnp.concatenate(rows[n], axis=0) for n in range(2)]

    p, alpha = [], []
    for n in range(2):
        mc = _class_allreduce(_fold_chunks(jnp.max(s[n], axis=0, keepdims=True), jnp.maximum), jnp.maximum, nh)
        m_old = m_scr[n]
        m_new = jnp.maximum(m_old, mc)
        a = jnp.exp(m_old - m_new)
        pn = jnp.exp(s[n] - _rep_chunks(m_new, w // LANES))
        l_scr[n] = l_scr[n] * _rep_chunks(a, w // LANES) + pn
        m_scr[n] = m_new
        p.append(pn)
        alpha.append(a)

    acc_c = jnp.zeros((N_MAPS, dv), F32)
    for i in range(pg):
        pm = jnp.where(sel[0], jnp.broadcast_to(p[0][i:i + 1, :], (N_MAPS, w)),
                       jnp.where(sel[1], jnp.broadcast_to(p[1][i:i + 1, :], (N_MAPS, w)), 0.0)).astype(BF16)
        v2 = v_refs[i][...].reshape(w, dv).astype(BF16)
        acc_c = acc_c + _dot(pm, v2)
    acc_scr[...] = acc_scr[...] * to_sub(alpha[0], alpha[1]) + acc_c

    @pl.when(j == nj - 1)
    def _():
        kn = kn_ref[...].astype(BF16).astype(F32)
        s_self = jnp.sum(q8.astype(BF16).astype(F32) * kn, axis=1, keepdims=True)
        a2, p_self, l_cls = [], [], []
        for n in range(2):
            s_cls = jnp.sum(jnp.where(sel1[n], jnp.broadcast_to(s_self, (N_MAPS, LANES)), 0.0),
                            axis=0, keepdims=True)
            m_new = m_scr[n]
            m_fin = jnp.maximum(m_new, s_cls)
            a = jnp.exp(m_new - m_fin)
            ps = jnp.exp(s_cls - m_fin)
            lc = _class_allreduce(_fold_chunks(jnp.sum(l_scr[n], axis=0, keepdims=True), jnp.add), jnp.add, nh)
            a2.append(a)
            p_self.append(ps)
            l_cls.append(lc * a + ps)
        o = (acc_scr[...] * to_sub(a2[0], a2[1]) + to_sub(p_self[0], p_self[1]) * vn_ref[...]) / to_sub(l_cls[0], l_cls[1])
        lam = _lambda_full(lq1_ref, lk1_ref, lq2_ref, lk2_ref, lam_init)
        d = o - lam * pltpu.roll(o, N_MAPS - 1, axis=0)
        y = d * lax.rsqrt(jnp.mean(d * d, axis=1, keepdims=True) + LN_EPS)
        o_ref[...] = (y * sub_ref[...]) * (1.0 - lam_init)


def _diff_dec_call(layer, page_table, q8, k_new8, v_new8, lam_vecs, subln, cache_k, cache_v, pg, lam_init):
    b = q8.shape[0]
    n_pages = page_table.shape[1]
    page = cache_k.shape[2]
    nj = n_pages // pg
    dv = DIFF_V

    def page_idx(i):
        return lambda bb, j, pt: (layer, pt[bb, n_pages - 1 - (j * pg + i)], 0, 0, 0)

    vec = pl.BlockSpec((1, HEAD_DIM), lambda bb, j, pt: (0, 0))
    tok = pl.BlockSpec((None, N_MAPS, HEAD_DIM), lambda bb, j, pt: (bb, 0, 0))
    in_specs = [tok, tok, pl.BlockSpec((None, N_MAPS, dv), lambda bb, j, pt: (bb, 0, 0)),
                vec, vec, vec, vec, pl.BlockSpec((1, dv), lambda bb, j, pt: (0, 0))]
    in_specs += [pl.BlockSpec((None, None, page, DIFF_HEADS, dv), page_idx(i)) for i in range(pg)]
    in_specs += [pl.BlockSpec((None, None, page, DIFF_HEADS, dv), page_idx(i)) for i in range(pg)]
    grid_spec = pltpu.PrefetchScalarGridSpec(
        num_scalar_prefetch=1,
        grid=(b, nj),
        in_specs=in_specs,
        out_specs=pl.BlockSpec((None, N_MAPS, dv), lambda bb, j, pt: (bb, 0, 0)),
        scratch_shapes=[
            pltpu.VMEM((2, 1, LANES), F32),
            pltpu.VMEM((2, pg, page * DIFF_HEADS), F32),
            pltpu.VMEM((N_MAPS, dv), F32),
        ],
    )
    return pl.pallas_call(
        functools.partial(_diff_dec_kernel, pg=pg, nj=nj, lam_init=lam_init),
        grid_spec=grid_spec,
        out_shape=SDS((b, N_MAPS, dv), F32),
        compiler_params=_cp("parallel", "arbitrary"),
        name="diff_decode",
    )(page_table, q8, k_new8, v_new8, *lam_vecs, subln, *([cache_k] * pg), *([cache_v] * pg))


def _mix_kernel(fo_ref, dn_ref, ga_ref, gb_ref, wf_ref, wd_ref, o_ref):
    yf = _dot(fo_ref[...].astype(BF16), wf_ref[...])
    yd = _dot(dn_ref[...].astype(BF16), wd_ref[...])
    o_ref[...] = (jax.nn.sigmoid(ga_ref[...]) * yf + jax.nn.sigmoid(gb_ref[...]) * yd).astype(o_ref.dtype)


def _mix_call(fo, dn, proj, wf, wd, tm):
    t = fo.shape[0]
    d = wf.shape[1]
    tn = 1024
    nb = d // tn
    return pl.pallas_call(
        _mix_kernel,
        grid=(nb, t // tm),
        in_specs=[
            pl.BlockSpec((tm, FOX_W), lambda j, i: (i, 0)),
            pl.BlockSpec((tm, FOX_W), lambda j, i: (i, 0)),
            pl.BlockSpec((tm, tn), lambda j, i: (i, 3 * nb + j)),
            pl.BlockSpec((tm, tn), lambda j, i: (i, 4 * nb + j)),
            pl.BlockSpec((FOX_W, tn), lambda j, i: (0, j)),
            pl.BlockSpec((FOX_W, tn), lambda j, i: (0, j)),
        ],
        out_specs=pl.BlockSpec((tm, tn), lambda j, i: (i, j)),
        out_shape=SDS((t, d), BF16),
        compiler_params=_cp("parallel", "parallel"),
        name="mix",
    )(fo, dn, proj, proj, wf, wd)


def _resid_ln_kernel(x_ref, y_ref, gate_ref, w_ref, g_ref, b_ref, o_ref, *, alpha):
    mix = _dot(y_ref[...], w_ref[...])
    z = alpha * x_ref[...] + gate_ref[...] * mix
    o_ref[...] = _ln(z) * g_ref[...] + b_ref[...]


def _mod_spec(r, tm, d):
    return pl.BlockSpec((1, d), lambda i: (0, 0)) if r == 1 else pl.BlockSpec((tm, d), lambda i: (i, 0))


def _resid_ln_call(x, y, gate, w, g, b, tm, alpha):
    t, d = x.shape
    row = pl.BlockSpec((tm, d), lambda i: (i, 0))
    vec = pl.BlockSpec((1, d), lambda i: (0, 0))
    return pl.pallas_call(
        functools.partial(_resid_ln_kernel, alpha=alpha),
        grid=(t // tm,),
        in_specs=[row, row, _mod_spec(gate.shape[0], tm, d), pl.BlockSpec((d, d), lambda i: (0, 0)), vec, vec],
        out_specs=row,
        out_shape=SDS((t, d), F32),
        compiler_params=_cp("parallel"),
        name="resid_ln1",
    )(x, y, gate, w, g, b)


def _final_ln_kernel(x_ref, ft_ref, gate_ref, g_ref, b_ref, o_ref, *, alpha):
    z = alpha * x_ref[...] + gate_ref[...] * ft_ref[...].T
    o_ref[...] = _ln(z) * g_ref[...] + b_ref[...]


def _final_ln_call(x, ffn_t, gate, g, b, tm, alpha):
    t, d = x.shape
    row = pl.BlockSpec((tm, d), lambda i: (i, 0))
    vec = pl.BlockSpec((1, d), lambda i: (0, 0))
    return pl.pallas_call(
        functools.partial(_final_ln_kernel, alpha=alpha),
        grid=(t // tm,),
        in_specs=[row, pl.BlockSpec((d, tm), lambda i: (0, i)), _mod_spec(gate.shape[0], tm, d), vec, vec],
        out_specs=row,
        out_shape=SDS((t, d), F32),
        compiler_params=_cp("parallel"),
        name="final_ln2",
    )(x, ffn_t, gate, g, b)


def _peer_q_kernel(x_ref, sc_ref, sh_ref, wt_ref, ht_ref, qt_ref):
    h = _ln(x_ref[...]) * (1.0 + sc_ref[...]) + sh_ref[...]
    ht = h.T.astype(BF16)
    ht_ref[...] = ht
    qt_ref[...] = _dot(wt_ref[...], ht)


def _peer_q_call(x, sc, sh, wqt, tm):
    t, d = x.shape
    n = wqt.shape[0]
    mod = _mod_spec(sc.shape[0], tm, d)
    return pl.pallas_call(
        _peer_q_kernel,
        grid=(t // tm,),
        in_specs=[pl.BlockSpec((tm, d), lambda i: (i, 0)), mod, mod, pl.BlockSpec((n, d), lambda i: (0, 0))],
        out_specs=[pl.BlockSpec((d, tm), lambda i: (0, i)), pl.BlockSpec((n, tm), lambda i: (0, i))],
        out_shape=[SDS((d, t), BF16), SDS((n, t), F32)],
        compiler_params=_cp("parallel"),
        name="peer_query",
    )(x, sc, sh, wqt)


def _top_rows(x, k, with_rank=False):
    cur = x
    vals = []
    rank = jnp.full(x.shape, float(k), F32) if with_rank else None
    for a in range(k):
        m = jnp.max(cur, axis=0, keepdims=True)
        vals.append(m)
        hit = cur == m
        if with_rank:
            rank = jnp.where(hit, float(a), rank)
        cur = jnp.where(hit, NEG_INF, cur)
    return (vals, rank) if with_rank else vals


def _peer_sel_kernel(qt_ref, k1_ref, k2_ref, r2_ref, e2_ref, nsel_ref, e1_ref):
    s1 = _dot(k1_ref[...].astype(BF16), qt_ref[:N_KEYS, :].astype(BF16))
    s2 = _dot(k2_ref[...].astype(BF16), qt_ref[N_KEYS:, :].astype(BF16))
    v1 = _top_rows(s1, PEER_TOPK)
    v2, rank2 = _top_rows(s2, PEER_TOPK, with_rank=True)
    v2s = jnp.concatenate(v2, axis=0)
    width = [PEER_TOPK // (a + 1) for a in range(PEER_TOPK)]
    blocks = [v1[a] + v2s[:width[a], :] for a in range(PEER_TOPK)]
    n_rows = sum(width)
    pad = -n_rows % 8
    cand = jnp.concatenate(blocks + [jnp.full((pad, s1.shape[1]), NEG_INF, F32)], axis=0)
    top = _top_rows(cand, PEER_TOPK)
    cmax, tau = top[0], top[-1]
    z = jnp.sum(jnp.where(cand >= tau, jnp.exp(cand - cmax), 0.0), axis=0, keepdims=True)
    nsel = jnp.zeros(s1.shape, F32)
    for a in range(PEER_TOPK):
        n_a = jnp.sum(jnp.where(blocks[a] >= tau, 1.0, 0.0), axis=0, keepdims=True)
        nsel = jnp.where(s1 == v1[a], n_a, nsel)
    r2_ref[...] = rank2.astype(BF16)
    e2_ref[...] = jnp.exp(s2 - v2[0]).astype(BF16)
    nsel_ref[...] = nsel
    e1_ref[...] = jnp.exp(s1 - v1[0]) / z


def _peer_sel_call(qt, k1, k2, tm):
    t = qt.shape[1]
    kd = 2 * N_KEYS
    key_spec = pl.BlockSpec((N_KEYS, N_KEYS), lambda i, h: (0, 0))
    st = pl.BlockSpec((None, N_KEYS, tm), lambda i, h: (h, 0, i))
    return pl.pallas_call(
        _peer_sel_kernel,
        grid=(t // tm, PEER_HEADS),
        in_specs=[pl.BlockSpec((kd, tm), lambda i, h: (h, i)), key_spec, key_spec],
        out_specs=[st, st, st, st],
        out_shape=[SDS((PEER_HEADS, N_KEYS, t), BF16), SDS((PEER_HEADS, N_KEYS, t), BF16),
                   SDS((PEER_HEADS, N_KEYS, t), F32), SDS((PEER_HEADS, N_KEYS, t), F32)],
        compiler_params=_cp("parallel", "arbitrary"),
        name="peer_select",
    )(qt, k1, k2)


GATE_ROWS = 16
ROW_BLOCK = 8


def _gelu(x):
    return 0.5 * x * (1.0 + lax.erf(x * (1.0 / math.sqrt(2.0))))


def _build_gates(r2_ref, e2_ref, nsel_rows_ref, e1_rows_ref, row0, dst_ref, r_lo, r_hi):
    tm = dst_ref.shape[1]
    n_tiles = N_KEYS // GATE_ROWS
    zero = jnp.zeros((GATE_ROWS, tm), BF16)
    for r in range(r_lo, r_hi):
        acc = [zero] * n_tiles
        for h in range(PEER_HEADS):
            n_b = jnp.broadcast_to(nsel_rows_ref[h, row0 + r:row0 + r + 1, :], (GATE_ROWS, tm)).astype(BF16)
            e1_b = jnp.broadcast_to(e1_rows_ref[h, row0 + r:row0 + r + 1, :], (GATE_ROWS, tm)).astype(BF16)
            for t in range(n_tiles):
                ks = slice(t * GATE_ROWS, (t + 1) * GATE_ROWS)
                acc[t] = acc[t] + jnp.where(r2_ref[h, ks, :] < n_b, e1_b * e2_ref[h, ks, :], zero)
        for t in range(n_tiles):
            dst_ref[r * N_KEYS + t * GATE_ROWS:r * N_KEYS + (t + 1) * GATE_ROWS, :] = acc[t]


def _peer_dense_kernel(h2t_ref, u_ref, vt_ref, r2_ref, e2_ref, nsel_rows_ref, e1_rows_ref, nsel_cur_ref, e1_cur_ref,
                       o_ref, g0_scr, g1_scr, *, te):
    j = pl.program_id(1)
    keys_per_step = te // N_KEYS
    n_chunks = te // MXU_DIM
    gate_refs = (r2_ref, e2_ref, nsel_rows_ref, e1_rows_ref)

    @pl.when(j == 0)
    def _():
        o_ref[...] = jnp.zeros_like(o_ref)
        _build_gates(r2_ref, e2_ref, nsel_cur_ref, e1_cur_ref, 0, g0_scr, 0, keys_per_step)

    def step(cur_scr, nxt_scr, nxt_row0):
        n_slices = 2 * n_chunks
        bounds = [keys_per_step * q // n_slices for q in range(n_slices + 1)]
        q = 0
        acts = []
        for c in range(n_chunks):
            rows = slice(c * MXU_DIM, (c + 1) * MXU_DIM)
            acts.append(_gelu(_dot(u_ref[rows, :], h2t_ref[...])).astype(BF16))
            _build_gates(*gate_refs, nxt_row0, nxt_scr, bounds[q], bounds[q + 1])
            q += 1
        upd = None
        for c in range(n_chunks):
            rows = slice(c * MXU_DIM, (c + 1) * MXU_DIM)
            g = cur_scr[rows, :] * acts[c]
            d = _dot(vt_ref[:, rows], g)
            upd = d if upd is None else upd + d
            _build_gates(*gate_refs, nxt_row0, nxt_scr, bounds[q], bounds[q + 1])
            q += 1
        o_ref[...] += upd

    odd_row0 = keys_per_step % ROW_BLOCK

    @pl.when(lax.rem(j, 2) == 0)
    def _():
        step(g0_scr, g1_scr, odd_row0)

    @pl.when(lax.rem(j, 2) == 1)
    def _():
        step(g1_scr, g0_scr, 0)


def _peer_dense_call(h2t, u_bf, vt_bf, r2, e2, nsel, e1, tm, te):
    d, t = h2t.shape
    ne = u_bf.shape[0]
    nj = ne // te
    keys_per_step = te // N_KEYS
    tiles_per_block = ROW_BLOCK // keys_per_step
    assert nj % 2 == 0 and tiles_per_block in (1, 2), "a sublane-aligned 8-row block of keys spans one or two tiles"
    st = pl.BlockSpec((PEER_HEADS, N_KEYS, tm), lambda i, j: (0, 0, i))
    nxt_rows = pl.BlockSpec((PEER_HEADS, ROW_BLOCK, tm), lambda i, j: (0, lax.rem(j + 1, nj) // tiles_per_block, i))
    cur_rows = pl.BlockSpec((PEER_HEADS, ROW_BLOCK, tm), lambda i, j: (0, 0, i))
    return pl.pallas_call(
        functools.partial(_peer_dense_kernel, te=te),
        grid=(t // tm, nj),
        in_specs=[
            pl.BlockSpec((d, tm), lambda i, j: (0, i)),
            pl.BlockSpec((te, d), lambda i, j: (j, 0)),
            pl.BlockSpec((d, te), lambda i, j: (0, j)),
            st, st, nxt_rows, nxt_rows, cur_rows, cur_rows,
        ],
        out_specs=pl.BlockSpec((d, tm), lambda i, j: (0, i)),
        out_shape=SDS((d, t), F32),
        scratch_shapes=[pltpu.VMEM((te, tm), BF16), pltpu.VMEM((te, tm), BF16)],
        compiler_params=_cp("parallel", "arbitrary"),
        name="peer_dense",
    )(h2t, u_bf, vt_bf, r2, e2, nsel, e1, nsel, e1)


def _rope_tables(pos):
    half = HEAD_DIM // 2
    inv = ROPE_THETA ** (-jnp.arange(half, dtype=F32) / half)
    ang = pos.astype(F32)[:, None] * inv[None, :]
    cos, sin = jnp.cos(ang), jnp.sin(ang)
    return jnp.concatenate([cos, cos], axis=1), jnp.concatenate([-sin, sin], axis=1)


def _layer_weights(l, w_in, b_f, w_o_fox, w_o_diff, w_out, peer_w_q, peer_u, peer_v):
    wl = w_in[l]
    c_fl = 3 * FOX_W
    w_r = jnp.concatenate(
        [wl[:, :c_fl], wl[:, c_fl + FOX_HEADS:], jnp.pad(wl[:, c_fl:c_fl + FOX_HEADS], ((0, 0), (0, LANES - FOX_HEADS)))],
        axis=1).astype(BF16)
    bf_pad = jnp.pad(b_f[l][None, :], ((0, 0), (0, LANES - FOX_HEADS)))
    return dict(
        w_r=w_r, bf_pad=bf_pad,
        wf=w_o_fox[l].astype(BF16), wd=w_o_diff[l].astype(BF16), wo=w_out[l].astype(BF16),
        pqt=peer_w_q[l].T.astype(BF16), u=peer_u[l].astype(BF16), vt=peer_v[l].T.astype(BF16),
    )


def _tail(x2d, mods, proj, fo, dn, lw, vecs, alpha, tm, peer_tm, peer_te):
    _, _, gate1, shift2, scale2, gate2 = mods
    t = x2d.shape[0]
    y = _mix_call(fo, dn, proj, lw["wf"], lw["wd"], 2 * tm if t % (2 * tm) == 0 else tm)
    x1 = _resid_ln_call(x2d, y, gate1, lw["wo"], vecs["ln1_g"], vecs["ln1_b"], tm, alpha)
    tp = -(-t // peer_tm) * peer_tm
    pad = tp - t
    x1p = jnp.pad(x1, ((0, pad), (0, 0))) if pad else x1
    sc2, sh2 = scale2, shift2
    if pad and sc2.shape[0] != 1:
        sc2 = jnp.pad(sc2, ((0, pad), (0, 0)))
        sh2 = jnp.pad(sh2, ((0, pad), (0, 0)))
    h2t, qt = _peer_q_call(x1p, sc2, sh2, lw["pqt"], min(peer_tm, 256))
    r2, e2, nsel, e1 = _peer_sel_call(qt, vecs["pk1"], vecs["pk2"], peer_tm)
    ffn_t = _peer_dense_call(h2t, lw["u"], lw["vt"], r2, e2, nsel, e1, peer_tm, peer_te)
    if pad:
        ffn_t = ffn_t[:, :t]
    return _final_ln_call(x1, ffn_t, gate2, vecs["ln2_g"], vecs["ln2_b"], tm, alpha)


def kernel(x_prompt, x_sample, cache_fox_k, cache_fox_v, cache_fox_logf, cache_diff_k, cache_diff_v, page_table, c_prompt, c_sample, w_ada, b_ada, w_in, b_f, lambda_q1, lambda_k1, lambda_q2, lambda_k2, diff_subln_w, w_o_fox, w_o_diff, w_out, ln1_g, ln1_b, peer_w_q, peer_sub_k1, peer_sub_k2, peer_u, peer_v, ln2_g, ln2_b):
    depth = w_ada.shape[0]
    bp, sp, d = x_prompt.shape
    bs, ss, _ = x_sample.shape
    assert bp == 1 and ss == 1, "one prompt sequence and one new token per sample sequence"
    n_pages = page_table.shape[1]
    page = cache_fox_k.shape[2]
    n_pool = cache_fox_k.shape[1]
    n_past = n_pages * page
    alpha = (2 * depth) ** 0.25
    dv = DIFF_V
    pg = 16

    cos_p, sin_p = _rope_tables(jnp.arange(sp))
    cos_s, sin_s = _rope_tables(n_past + jnp.arange(ss))
    lf_flat = cache_fox_logf.reshape(depth, n_pool, 1, page * FOX_HEADS)

    c_all = jnp.concatenate([c_prompt, c_sample], axis=0)
    n_c = c_all.shape[0]
    c_all = jnp.pad(c_all, ((0, -n_c % 8), (0, 0)))

    xp = x_prompt.reshape(sp, d)
    xs = x_sample.reshape(bs, d)
    st_p, st_s = [], []
    for l in range(depth):
        lam_init = 0.8 - 0.6 * math.exp(-0.3 * l)
        lw = _layer_weights(l, w_in, b_f, w_o_fox, w_o_diff, w_out, peer_w_q, peer_u, peer_v)
        vecs = dict(ln1_g=ln1_g[l][None], ln1_b=ln1_b[l][None], ln2_g=ln2_g[l][None], ln2_b=ln2_b[l][None],
                    pk1=peer_sub_k1[l], pk2=peer_sub_k2[l])
        lam_vecs = [v[l][None] for v in (lambda_q1, lambda_k1, lambda_q2, lambda_k2)]
        subln = diff_subln_w[l][None]
        subln_rep = jnp.broadcast_to(diff_subln_w[l][:, None], (dv, LANES))

        mod = _ada_call(c_all, w_ada, b_ada, l)
        mods_p = [mod[0:bp, i * d:(i + 1) * d] for i in range(6)]
        mods_s = [mod[bp:bp + bs, i * d:(i + 1) * d] for i in range(6)]

        proj = _lnmod_mm_call(xp, mods_p[1], mods_p[0], lw["w_r"], 1024, 1152)
        (fk_o, fv_o, dk_o, dv_o, logf, fqt, fkb, fvt, dqt, dkb, dvt, ckb, cumt) = _post_prompt_call(
            proj, cos_p, sin_p, lw["bf_pad"], 256)
        fo = _fox_flash_call(fqt, fkb, fvt, cumt, ckb, 512)
        dn = _diff_flash_call(dqt, dkb, dvt, lam_vecs, subln_rep, 512, lam_init)
        st_p.append((fk_o.reshape(bp, sp, FOX_HEADS, HEAD_DIM), fv_o.reshape(bp, sp, FOX_HEADS, HEAD_DIM),
                     logf.reshape(bp, sp, FOX_HEADS),
                     dk_o.reshape(bp, sp, DIFF_HEADS, dv), dv_o.reshape(bp, sp, DIFF_HEADS, dv)))
        xp = _tail(xp, mods_p, proj, fo, dn, lw, vecs, alpha, 256, 512, 1024)

        proj_s = _lnmod_mm_call(xs, mods_s[1], mods_s[0], lw["w_r"], bs, 1152)
        dq_s, dk_s, logf_s = _post_call(proj_s, cos_s, sin_s, lw["bf_pad"], bs)
        fk_s = proj_s[:, FOX_W:2 * FOX_W].reshape(bs, FOX_HEADS, HEAD_DIM)
        fv_s = proj_s[:, 2 * FOX_W:3 * FOX_W].reshape(bs, FOX_HEADS, HEAD_DIM)
        dv_s = proj_s[:, 5 * FOX_W:6 * FOX_W].reshape(bs, DIFF_HEADS, dv)
        lfn_cls = jnp.tile(logf_s, (1, LANES // FOX_HEADS)).reshape(bs, 1, LANES)
        fo_s = _fox_dec_call(l, page_table, proj_s[:, :FOX_W].reshape(bs, FOX_HEADS, HEAD_DIM), fk_s, fv_s, lfn_cls,
                             cache_fox_k, cache_fox_v, lf_flat, pg)
        dn_s = _diff_dec_call(l, page_table, dq_s.reshape(bs, N_MAPS, HEAD_DIM), dk_s.reshape(bs, N_MAPS, HEAD_DIM),
                              jnp.repeat(dv_s, 2, axis=1), lam_vecs, subln, cache_diff_k, cache_diff_v, pg, lam_init)
        dn_s = dn_s[:, ::2, :].reshape(bs, FOX_W)
        st_s.append((fk_s.reshape(bs, ss, FOX_HEADS, HEAD_DIM), fv_s.reshape(bs, ss, FOX_HEADS, HEAD_DIM),
                     logf_s.reshape(bs, ss, FOX_HEADS), dk_s.reshape(bs, ss, DIFF_HEADS, dv),
                     dv_s.reshape(bs, ss, DIFF_HEADS, dv)))
        xs = _tail(xs, mods_s, proj_s, fo_s.reshape(bs, FOX_W), dn_s, lw, vecs, alpha, bs, LANES, 1024)

    outs_p = [jnp.stack([s[i] for s in st_p]) for i in range(5)]
    outs_s = [jnp.stack([s[i] for s in st_s]) for i in range(5)]
    return (xp.reshape(bp, sp, d), xs.reshape(bs, ss, d), *outs_p, *outs_s)
```
